```python
import math
import jax, jax.numpy as jnp
from jax import lax
import numpy as np

D_MODEL = 1024
BATCH = 32
SEQ = 2048
DEPTH = 1
DEC_BATCH = 128
DEC_SEQ = 8
PAST_LEN = 8192
PAGE_SIZE = 128

SB_HEADS = 8
SB_HEAD_DIM = 64
SB_WIDTH = SB_HEADS * SB_HEAD_DIM
SB_Q_BLOCK = 128
SB_BIAS_HI = -1.0
SB_BIAS_LO = -9.0
GLA_HEADS = 4
GLA_DK = 64
GLA_DV = 128
GLA_KWIDTH = GLA_HEADS * GLA_DK
GLA_VWIDTH = GLA_HEADS * GLA_DV
GLA_GATE_RANK = 16
GLA_GATE_TEMP = 16.0
GLA_CHUNK = 64
PEER_HEADS = 8
PEER_N_KEYS = 128
PEER_N_EXPERTS = PEER_N_KEYS * PEER_N_KEYS
PEER_KEY_HALF = 128
PEER_TOPK = 16
PEER_BLOCK = 256
NORM_EPS = 1e-6

IN_SPLITS = [SB_WIDTH, SB_WIDTH, SB_WIDTH,
             GLA_KWIDTH, GLA_KWIDTH, GLA_VWIDTH, GLA_VWIDTH,
             GLA_GATE_RANK,
             D_MODEL, D_MODEL]
IN_WIDTH = sum(IN_SPLITS)

kernel_name = 'sb_gla_peer_hybrid_step'

F32 = jnp.float32


def rmsnorm(x, g):
    xf = x.astype(F32)
    y = xf * lax.rsqrt(jnp.mean(xf * xf, axis=-1, keepdims=True) + NORM_EPS)
    return (y * g.astype(F32)).astype(x.dtype)


def split_in_proj(h, w_in):
    z = jnp.einsum('bld,de->ble', h, w_in)
    offs = [int(o) for o in np.cumsum(IN_SPLITS)[:-1]]
    return jnp.split(z, offs, axis=-1)


def sb_attend_block(q, k, v, bias, q_pos, k_pos):
    z = (jnp.einsum('bqhd,bkhd->bhqk', q.astype(F32), k.astype(F32)) * (SB_HEAD_DIM ** -0.5)
         + bias.astype(F32)[None, :, None, None])
    valid = (k_pos[None, :] < q_pos[:, None])[None, None]
    log_surv = jnp.where(valid, jax.nn.log_sigmoid(-z), 0.0)
    after = lax.cumsum(log_surv, axis=3, reverse=True) - log_surv
    w = jnp.where(valid, jnp.exp(jax.nn.log_sigmoid(z) + after), 0.0)
    return jnp.einsum('bhqk,bkhd->bqhd', w, v.astype(F32)).astype(v.dtype)


def sb_prompt(q, k, v, bias):
    B, L = q.shape[:2]
    blk = math.gcd(SB_Q_BLOCK, L)
    nb = L // blk
    pos = jnp.arange(L, dtype=jnp.int32)
    qb = q.reshape(B, nb, blk, SB_HEADS, SB_HEAD_DIM).swapaxes(0, 1)
    pb = pos.reshape(nb, blk)
    out = lax.map(lambda a: sb_attend_block(a[0], k, v, bias, a[1], pos), (qb, pb))
    return out.swapaxes(0, 1).reshape(B, L, SB_HEADS, SB_HEAD_DIM)


def sb_sample(q, k_new, v_new, bias, cache_k, cache_v, page_table):
    Bd, Ln = q.shape[:2]
    past_len = page_table.shape[1] * cache_k.shape[1]
    past_k = cache_k[page_table].reshape(Bd, past_len, SB_HEADS, SB_HEAD_DIM)
    past_v = cache_v[page_table].reshape(Bd, past_len, SB_HEADS, SB_HEAD_DIM)
    k_all = jnp.concatenate([past_k, k_new.astype(past_k.dtype)], axis=1)
    v_all = jnp.concatenate([past_v, v_new.astype(past_v.dtype)], axis=1)
    k_pos = jnp.arange(past_len + Ln, dtype=jnp.int32)
    q_pos = past_len + jnp.arange(Ln, dtype=jnp.int32)
    return sb_attend_block(q, k_all, v_all, bias, q_pos, k_pos)


def gla_scan(q, k, v, log_a, S0):
    B, L = q.shape[:2]
    C = math.gcd(GLA_CHUNK, L)
    n = L // C

    def chunks(t):
        return t.astype(F32).reshape(B, n, C, *t.shape[2:]).swapaxes(0, 1)

    causal = jnp.tril(jnp.ones((C, C), dtype=bool))[None, :, :, None, None]

    def step(S, inp):
        qc, kc, vc, lac = inp
        b = jnp.cumsum(lac, axis=1)
        o_inter = jnp.einsum('bthd,bhde->bthe', qc * jnp.exp(b), S)
        decay = jnp.exp(jnp.where(causal, b[:, :, None] - b[:, None, :], -jnp.inf))
        att = jnp.einsum('bthd,bshd,btshd->bhts', qc, kc, decay)
        o_intra = jnp.einsum('bhts,bshe->bthe', att, vc)
        b_end = b[:, -1]
        S_new = jnp.exp(b_end)[..., None] * S + jnp.einsum(
            'bshd,bshe->bhde', kc * jnp.exp(b_end[:, None] - b), vc)
        return S_new, o_inter + o_intra

    S_fin, o = lax.scan(step, S0.astype(F32), (chunks(q), chunks(k), chunks(v), chunks(log_a)))
    return o.swapaxes(0, 1).reshape(B, L, GLA_HEADS, GLA_DV), S_fin


def mixer(h, sb_fn, S0, w_in, sb_bias, gla_gate_w2, gla_gate_b, gla_norm_g, w_sb_up, w_gla_up, w_out):
    B, L = h.shape[:2]
    q_a, k_a, v_a, q_b, k_b, v_b, r_b, gk_lr, gate_a, gate_b = split_in_proj(h, w_in)

    def heads(t, nh):
        return t.reshape(B, L, nh, -1)

    ka, va = heads(k_a, SB_HEADS), heads(v_a, SB_HEADS)
    o_a = sb_fn(heads(q_a, SB_HEADS), ka, va, sb_bias).reshape(B, L, SB_WIDTH)
    log_a = jax.nn.log_sigmoid((jnp.einsum('blr,rk->blk', gk_lr, gla_gate_w2) + gla_gate_b).astype(F32)) / GLA_GATE_TEMP
    o_b, S_fin = gla_scan(heads(q_b, GLA_HEADS) * (GLA_DK ** -0.5), heads(k_b, GLA_HEADS),
                          heads(v_b, GLA_HEADS), heads(log_a, GLA_HEADS), S0)
    o_b = rmsnorm(o_b, gla_norm_g).reshape(B, L, GLA_VWIDTH).astype(h.dtype) * jax.nn.silu(r_b)
    merged = (jax.nn.sigmoid(gate_a) * jnp.einsum('ble,ed->bld', o_a, w_sb_up)
              + jax.nn.sigmoid(gate_b) * jnp.einsum('ble,ed->bld', o_b, w_gla_up))
    y = jnp.einsum('bld,de->ble', merged, w_out)
    return y, ka, va, S_fin


def peer(h, w_q, keys, u_tab, v_tab):
    B, L, D = h.shape
    n = B * L
    nblk = -(-n // PEER_BLOCK)
    xt = jnp.pad(h.reshape(n, D), ((0, nblk * PEER_BLOCK - n), (0, 0))).reshape(nblk, PEER_BLOCK, D)

    def block(xb):
        qry = jnp.einsum('nd,de->ne', xb, w_q).reshape(-1, PEER_HEADS, 2, PEER_KEY_HALF)
        s = jnp.einsum('nhpd,hpkd->nhpk', qry.astype(F32), keys.astype(F32))
        sv, si = lax.top_k(s, PEER_TOPK)
        cand_s = (sv[:, :, 0, :, None] + sv[:, :, 1, None, :]).reshape(-1, PEER_HEADS, PEER_TOPK * PEER_TOPK)
        cand_i = (si[:, :, 0, :, None] * PEER_N_KEYS + si[:, :, 1, None, :]).reshape(-1, PEER_HEADS, PEER_TOPK * PEER_TOPK)
        top_s, top_p = lax.top_k(cand_s, PEER_TOPK)
        ids = jnp.take_along_axis(cand_i, top_p, axis=-1)
        g = jax.nn.softmax(top_s, axis=-1)
        act = jax.nn.gelu(jnp.einsum('nd,nhkd->nhk', xb, u_tab[ids]).astype(F32), approximate=False)
        return jnp.einsum('nhk,nhkd->nd', (g * act).astype(xb.dtype), v_tab[ids])

    out = lax.map(block, xt).reshape(-1, D)[:n]
    return out.reshape(B, L, D)


def setup_inputs(seed: int = 0) -> dict:
    key = jax.random.key(seed)
    ks = jax.random.split(key, 24)
    n_pages = PAST_LEN // PAGE_SIZE
    n_used = DEC_BATCH * n_pages
    n_pool = n_used + n_used // 4

    def nrm(k, shape, scale):
        return jax.random.normal(k, shape, F32) * scale

    page_table = jax.random.permutation(ks[0], n_pool)[:n_used].reshape(DEC_BATCH, n_pages).astype(jnp.int32)
    sb_bias = (jnp.linspace(SB_BIAS_HI, SB_BIAS_LO, SB_HEADS, dtype=F32)[None, :]
               + nrm(ks[20], (DEPTH, SB_HEADS), 0.1))
    return {
        'x_prompt': nrm(ks[1], (BATCH, SEQ, D_MODEL), 1.0),
        'x_sample': nrm(ks[2], (DEC_BATCH, DEC_SEQ, D_MODEL), 1.0),
        'cache_sb_k': nrm(ks[3], (DEPTH, n_pool, PAGE_SIZE, SB_HEADS, SB_HEAD_DIM), 1.0),
        'cache_sb_v': nrm(ks[4], (DEPTH, n_pool, PAGE_SIZE, SB_HEADS, SB_HEAD_DIM), 1.0),
        'page_table': page_table,
        'state_gla': nrm(ks[5], (DEPTH, DEC_BATCH, GLA_HEADS, GLA_DK, GLA_DV), 0.5),
        'norm_mix_g': 1.0 + nrm(ks[6], (DEPTH, D_MODEL), 0.02),
        'w_in': nrm(ks[7], (DEPTH, D_MODEL, IN_WIDTH), D_MODEL ** -0.5),
        'sb_bias': sb_bias,
        'gla_gate_w2': nrm(ks[8], (DEPTH, GLA_GATE_RANK, GLA_KWIDTH), GLA_GATE_RANK ** -0.5),
        'gla_gate_b': nrm(ks[9], (DEPTH, GLA_KWIDTH), 0.1),
        'gla_norm_g': 1.0 + nrm(ks[10], (DEPTH, GLA_HEADS, GLA_DV), 0.02),
        'w_sb_up': nrm(ks[11], (DEPTH, SB_WIDTH, D_MODEL), SB_WIDTH ** -0.5),
        'w_gla_up': nrm(ks[12], (DEPTH, GLA_VWIDTH, D_MODEL), GLA_VWIDTH ** -0.5),
        'w_out': nrm(ks[13], (DEPTH, D_MODEL, D_MODEL), D_MODEL ** -0.5),
        'norm_ffn_g': 1.0 + nrm(ks[14], (DEPTH, D_MODEL), 0.02),
        'peer_w_q': nrm(ks[15], (DEPTH, D_MODEL, PEER_HEADS * 2 * PEER_KEY_HALF), D_MODEL ** -0.5),
        'peer_keys': nrm(ks[16], (DEPTH, PEER_HEADS, 2, PEER_N_KEYS, PEER_KEY_HALF), PEER_KEY_HALF ** -0.5),
        'peer_u': nrm(ks[17], (DEPTH, PEER_N_EXPERTS, D_MODEL), D_MODEL ** -0.5),
        'peer_v': nrm(ks[18], (DEPTH, PEER_N_EXPERTS, D_MODEL), PEER_HEADS ** -0.5),
        'norm_final_g': 1.0 + nrm(ks[19], (D_MODEL,), 0.02),
    }


def reference(x_prompt, x_sample, cache_sb_k, cache_sb_v, page_table, state_gla,
              norm_mix_g, w_in, sb_bias, gla_gate_w2, gla_gate_b, gla_norm_g, w_sb_up, w_gla_up, w_out,
              norm_ffn_g, peer_w_q, peer_keys, peer_u, peer_v, norm_final_g):
    xp, xs = x_prompt, x_sample
    kp_l, vp_l, sp_l, ks_l, vs_l, ss_l = [], [], [], [], [], []
    for l in range(DEPTH):
        mix_w = (w_in[l], sb_bias[l], gla_gate_w2[l], gla_gate_b[l], gla_norm_g[l], w_sb_up[l], w_gla_up[l], w_out[l])
        peer_w = (peer_w_q[l], peer_keys[l], peer_u[l], peer_v[l])
        ck, cv = cache_sb_k[l], cache_sb_v[l]

        S0p = jnp.zeros((xp.shape[0], GLA_HEADS, GLA_DK, GLA_DV), F32)
        yp, kp, vp, Sp = mixer(rmsnorm(xp, norm_mix_g[l]), sb_prompt, S0p, *mix_w)
        xp = xp + yp
        xp = xp + peer(rmsnorm(xp, norm_ffn_g[l]), *peer_w)

        def sb_fn(q, k, v, bias):
            return sb_sample(q, k, v, bias, ck, cv, page_table)
        ys, k_s, v_s, Ss = mixer(rmsnorm(xs, norm_mix_g[l]), sb_fn, state_gla[l], *mix_w)
        xs = xs + ys
        xs = xs + peer(rmsnorm(xs, norm_ffn_g[l]), *peer_w)

        kp_l.append(kp); vp_l.append(vp); sp_l.append(Sp.astype(xp.dtype))
        ks_l.append(k_s); vs_l.append(v_s); ss_l.append(Ss.astype(xs.dtype))

    y_prompt = rmsnorm(xp, norm_final_g)
    y_sample = rmsnorm(xs, norm_final_g)
    k_prompt = jnp.stack(kp_l)
    v_prompt = jnp.stack(vp_l)
    gla_prompt = jnp.stack(sp_l)
    k_sample = jnp.stack(ks_l)
    v_sample = jnp.stack(vs_l)
    gla_sample = jnp.stack(ss_l)
    return (y_prompt, y_sample, k_prompt, v_prompt, gla_prompt, k_sample, v_sample, gla_sample)
```

```python
import functools

import numpy as np
import jax
import jax.numpy as jnp
from jax import lax
from jax.experimental import pallas as pl
from jax.experimental.pallas import tpu as pltpu

F32 = jnp.float32
BF16 = jnp.bfloat16

NORM_EPS = 1e-6
SB_HEADS = 8
SB_HEAD_DIM = 64
SB_WIDTH = SB_HEADS * SB_HEAD_DIM
GLA_HEADS = 4
GLA_DK = 64
GLA_DV = 128
GLA_KWIDTH = GLA_HEADS * GLA_DK
GLA_VWIDTH = GLA_HEADS * GLA_DV
GLA_GATE_RANK = 16
GLA_GATE_TEMP = 16.0
PEER_HEADS = 8
PEER_N_KEYS = 128
PEER_KEY_HALF = 128
PEER_TOPK = 16

LANES = 128
SUBLANES = 8
VMEM_LIMIT_BYTES = 56 * 1024 * 1024
GLA_CHUNK = 128
GLA_LEVELS = 7
SB_TILE = 128
SB_PAGES_PER_STEP = 8

_NT = (((1,), (1,)), ((), ()))
_TN = (((0,), (0,)), ((), ()))


def _cparams(sem):
    return pltpu.CompilerParams(dimension_semantics=sem, vmem_limit_bytes=VMEM_LIMIT_BYTES)


def _rmsnorm(x, g):
    return x * lax.rsqrt(jnp.mean(x * x, axis=-1, keepdims=True) + NORM_EPS) * g


def _neg_softplus(z):
    return -(jnp.maximum(z, 0.0) + jnp.log1p(jnp.exp(-jnp.abs(z))))


def _split2(x):
    hi = x.astype(BF16)
    lo = (x - hi.astype(F32)).astype(BF16)
    return hi, lo


def _split3(x):
    hi = x.astype(BF16)
    r = x - hi.astype(F32)
    mid = r.astype(BF16)
    lo = (r - mid.astype(F32)).astype(BF16)
    return hi, mid, lo


def _inproj_kernel(x_ref, g_ref, w_ref, wlr_ref, w2_ref, b2_ref,
                   qa_ref, ka_ref, va_ref, qkb_ref, vb_ref, rb_ref, la_ref, gates_ref):
    h = _rmsnorm(x_ref[...], g_ref[...]).astype(BF16)

    def proj(lo, hi):
        return jnp.dot(h, w_ref[:, lo:hi], preferred_element_type=F32)

    o = 0
    qa_ref[...] = proj(o, o + SB_WIDTH) * (SB_HEAD_DIM ** -0.5); o += SB_WIDTH
    ka_ref[...] = proj(o, o + SB_WIDTH); o += SB_WIDTH
    va_ref[...] = proj(o, o + SB_WIDTH); o += SB_WIDTH
    qkb_ref[:, :GLA_KWIDTH] = proj(o, o + GLA_KWIDTH) * (GLA_DK ** -0.5); o += GLA_KWIDTH
    qkb_ref[:, GLA_KWIDTH:] = proj(o, o + GLA_KWIDTH); o += GLA_KWIDTH
    vb_ref[...] = proj(o, o + GLA_VWIDTH); o += GLA_VWIDTH
    rb_ref[...] = proj(o, o + GLA_VWIDTH); o += GLA_VWIDTH
    d = gates_ref.shape[1]
    gates_ref[...] = proj(o, o + d)
    gk = jnp.dot(h, wlr_ref[...], preferred_element_type=F32)
    pre = jnp.dot(gk, w2_ref[...], preferred_element_type=F32,
                  precision=lax.Precision.HIGHEST) + b2_ref[...]
    la_ref[...] = _neg_softplus(-pre) * (1.0 / GLA_GATE_TEMP)


def _in_proj(x2d, g, w_main, w_lr, w2, b2, tm):
    n, d = x2d.shape
    wm = w_main.shape[1]
    const = lambda i: (0, 0)
    row = lambda i: (i, 0)
    widths = [SB_WIDTH, SB_WIDTH, SB_WIDTH, 2 * GLA_KWIDTH, GLA_VWIDTH, GLA_VWIDTH, GLA_KWIDTH, 2 * d]
    return pl.pallas_call(
        _inproj_kernel,
        grid=(n // tm,),
        in_specs=[pl.BlockSpec((tm, d), row),
                  pl.BlockSpec((1, d), const),
                  pl.BlockSpec((d, wm), const),
                  pl.BlockSpec((d, LANES), const),
                  pl.BlockSpec((LANES, GLA_KWIDTH), const),
                  pl.BlockSpec((1, GLA_KWIDTH), const)],
        out_specs=[pl.BlockSpec((tm, w), row) for w in widths],
        out_shape=[jax.ShapeDtypeStruct((n, w), F32) for w in widths],
        compiler_params=_cparams(("parallel",)),
        name="in_proj",
    )(x2d, g, w_main, w_lr, w2, b2)


def _sb_prompt_kernel(bias_ref, q_ref, k_ref, v_ref, o_ref, *, t):
    p = pl.program_id(1)
    qi = pl.program_id(2)
    lane = lax.broadcasted_iota(jnp.int32, (1, LANES), 1)
    first = lane < SB_HEAD_DIM
    q = q_ref[0]
    qh = (jnp.where(first, q, 0.0).astype(BF16), jnp.where(first, 0.0, q).astype(BF16))
    bias = (bias_ref[2 * p], bias_ref[2 * p + 1])
    row = lax.broadcasted_iota(jnp.int32, (t, t), 0)
    col = lax.broadcasted_iota(jnp.int32, (t, t), 1)
    valid = col < row
    cmat = jnp.concatenate([jnp.where(row > col, 1.0, 0.0), jnp.ones((t, t), F32)], axis=1).astype(BF16)

    def block(j, carry, masked):
        acc, c0, c1 = carry
        off = pl.multiple_of(j * t, t)
        kt = k_ref[0, pl.ds(off, t), :].astype(BF16)
        vt = v_ref[0, pl.ds(off, t), :]
        vh = (jnp.where(first, vt, 0.0).astype(BF16), jnp.where(first, 0.0, vt).astype(BF16))
        cs = [c0, c1]
        for hh in range(2):
            z = lax.dot_general(qh[hh], kt, _NT, preferred_element_type=F32) + bias[hh]
            ls = _neg_softplus(z)
            lsig = z + ls
            if masked:
                ls = jnp.where(valid, ls, 0.0)
            hi, lo = _split2(ls)
            r = (jnp.dot(hi, cmat, preferred_element_type=F32)
                 + jnp.dot(lo, cmat, preferred_element_type=F32))
            w = jnp.exp(lsig + r[:, :t] + cs[hh])
            if masked:
                w = jnp.where(valid, w, 0.0)
            acc = acc + jnp.dot(w.astype(BF16), vh[hh], preferred_element_type=F32)
            cs[hh] = cs[hh] + r[:, t:]
        return acc, cs[0], cs[1]

    carry = (jnp.zeros((t, LANES), F32), jnp.zeros((t, t), F32), jnp.zeros((t, t), F32))
    carry = block(qi, carry, True)
    carry = lax.fori_loop(0, qi, lambda jj, c: block(qi - 1 - jj, c, False), carry)
    o_ref[0] = carry[0]


def _sb_prompt(q, k, v, bias):
    b, l, _ = q.shape
    t = SB_TILE
    return pl.pallas_call(
        functools.partial(_sb_prompt_kernel, t=t),
        grid=(b, SB_WIDTH // LANES, l // t),
        in_specs=[pl.BlockSpec(memory_space=pltpu.SMEM),
                  pl.BlockSpec((1, t, LANES), lambda bi, p, qi: (bi, qi, p)),
                  pl.BlockSpec((1, l, LANES), lambda bi, p, qi: (bi, 0, p)),
                  pl.BlockSpec((1, l, LANES), lambda bi, p, qi: (bi, 0, p))],
        out_specs=pl.BlockSpec((1, t, LANES), lambda bi, p, qi: (bi, qi, p)),
        out_shape=jax.ShapeDtypeStruct((b, l, SB_WIDTH), F32),
        compiler_params=_cparams(("parallel", "parallel", "arbitrary")),
        name="sb_prompt",
    )(bias, q, k, v)


def _sb_sample_kernel(pt_ref, brow_ref, qbd_ref, kn_ref, vn_ref, *rest, npg, ln):
    del pt_ref
    kp_refs = rest[:npg]
    vp_refs = rest[npg:2 * npg]
    o_ref, acc_ref, carry_ref = rest[2 * npg:]
    g = pl.program_id(1)
    pg = kp_refs[0].shape[1]
    qbd = qbd_ref[0]
    brow = brow_ref[...]
    row = lax.broadcasted_iota(jnp.int32, (pg, pg), 0)
    col = lax.broadcasted_iota(jnp.int32, (pg, pg), 1)
    later = jnp.where(col > row, 1.0, 0.0).astype(BF16)

    def page(kp, vp, valid):
        z = jnp.dot(kp.astype(BF16), qbd, preferred_element_type=F32) + brow
        ls = _neg_softplus(z)
        lsig = z + ls
        if valid is not None:
            ls = jnp.where(valid, ls, 0.0)
        hi, lo = _split2(ls)
        aft = (jnp.dot(later, hi, preferred_element_type=F32)
               + jnp.dot(later, lo, preferred_element_type=F32))
        w = jnp.exp(lsig + aft + carry_ref[...])
        if valid is not None:
            w = jnp.where(valid, w, 0.0)
        acc_ref[...] += jnp.dot(w.T.astype(BF16), vp.astype(BF16), preferred_element_type=F32)
        carry_ref[...] += jnp.sum(ls, axis=0, keepdims=True)

    @pl.when(g == 0)
    def _():
        acc_ref[...] = jnp.zeros_like(acc_ref)
        carry_ref[...] = jnp.zeros_like(carry_ref)
        pad = jnp.zeros((pg - ln, SB_WIDTH), F32)
        kn = jnp.concatenate([kn_ref[0], pad], axis=0)
        vn = jnp.concatenate([vn_ref[0], pad], axis=0)
        srow = lax.broadcasted_iota(jnp.int32, (pg, LANES), 0)
        tcol = lax.broadcasted_iota(jnp.int32, (pg, LANES), 1) % ln
        page(kn, vn, srow < tcol)

    for i in range(npg):
        page(kp_refs[i][0], vp_refs[i][0], None)

    @pl.when(g == pl.num_programs(1) - 1)
    def _():
        lane = lax.broadcasted_iota(jnp.int32, (ln, SB_WIDTH), 1)
        out = jnp.zeros((ln, SB_WIDTH), F32)
        for h in range(SB_HEADS):
            out = out + jnp.where(lane // SB_HEAD_DIM == h, acc_ref[h * ln:(h + 1) * ln, :], 0.0)
        o_ref[0] = out


def _sb_sample(q, k_new, v_new, bias, cache_k, cache_v, page_table):
    bd, ln, _ = q.shape
    n_pages = page_table.shape[1]
    pg = cache_k.shape[1]
    npg = SB_PAGES_PER_STEP
    while n_pages % npg:
        npg //= 2
    assert ln == SUBLANES and SB_HEADS * ln <= LANES and pg % LANES == 0
    q4 = q.reshape(bd, ln, SB_HEADS, SB_HEAD_DIM)
    qbd = jnp.einsum('bthd,hg->bhdgt', q4, jnp.eye(SB_HEADS, dtype=F32))
    qbd = qbd.reshape(bd, SB_WIDTH, SB_HEADS * ln)
    qbd = jnp.pad(qbd, ((0, 0), (0, 0), (0, LANES - SB_HEADS * ln))).astype(BF16)
    brow = jnp.pad(jnp.repeat(bias, ln), (0, LANES - SB_HEADS * ln)).reshape(1, LANES)

    def page_spec(i):
        return pl.BlockSpec((1, pg, SB_WIDTH),
                            lambda b, g, pt, i=i: (pt[b, n_pages - 1 - (g * npg + i)], 0, 0))

    seq = lambda b, g, pt: (b, 0, 0)
    grid_spec = pltpu.PrefetchScalarGridSpec(
        num_scalar_prefetch=1,
        grid=(bd, n_pages // npg),
        in_specs=[pl.BlockSpec((1, LANES), lambda b, g, pt: (0, 0)),
                  pl.BlockSpec((1, SB_WIDTH, LANES), seq),
                  pl.BlockSpec((1, ln, SB_WIDTH), seq),
                  pl.BlockSpec((1, ln, SB_WIDTH), seq)]
                 + [page_spec(i) for i in range(npg)] * 2,
        out_specs=pl.BlockSpec((1, ln, SB_WIDTH), seq),
        scratch_shapes=[pltpu.VMEM((LANES, SB_WIDTH), F32), pltpu.VMEM((1, LANES), F32)],
    )
    return pl.pallas_call(
        functools.partial(_sb_sample_kernel, npg=npg, ln=ln),
        grid_spec=grid_spec,
        out_shape=jax.ShapeDtypeStruct((bd, ln, SB_WIDTH), F32),
        compiler_params=_cparams(("parallel", "arbitrary")),
        name="sb_sample",
    )(page_table, brow, qbd, k_new, v_new, *([cache_k] * npg), *([cache_v] * npg))


def _gla_constants():
    c = GLA_CHUNK
    t = np.arange(c)[:, None]
    u = np.arange(c)[None, :]
    blocks = [(u <= t), (u > t)]
    level = np.zeros((c, c), np.int32)
    for l in range(1, GLA_LEVELS + 1):
        m = c >> l
        r = (t // (2 * m)) * (2 * m) + m - 1
        upper = (t % (2 * m)) >= m
        blocks.append(upper & (u > r) & (u <= t))
        blocks.append((~upper) & (u > t) & (u <= r))
        tt, ss = np.arange(c)[:, None], np.arange(c)[None, :]
        same = (tt // (2 * m)) == (ss // (2 * m))
        level[same & ((tt % (2 * m)) >= m) & ((ss % (2 * m)) < m)] = l
    level[np.arange(c), np.arange(c)] = GLA_LEVELS + 1
    mall = np.concatenate([b.astype(np.float32) for b in blocks], axis=0)
    return jnp.asarray(mall, BF16), jnp.asarray(level)


def _gla_kernel(mall_ref, level_ref, qk_ref, v_ref, r_ref, la_ref, s0_ref, gn_ref,
                o_ref, sfin_ref, s_scr):
    c = GLA_CHUNK
    lb = qk_ref.shape[1]
    ci = pl.program_id(1)

    @pl.when(ci == 0)
    def _():
        s_scr[...] = s0_ref[0]

    def rows(x):
        if lb == c:
            return x
        return jnp.concatenate([x, jnp.zeros((c - lb, x.shape[1]), x.dtype)], axis=0)

    qk = rows(qk_ref[0])
    q = qk[:, :GLA_KWIDTH]
    k = qk[:, GLA_KWIDTH:]
    v = rows(v_ref[0])
    la = rows(la_ref[0])
    hi, mid, lo = _split3(la)
    mall = mall_ref[...]
    cum = (jnp.dot(mall, hi, preferred_element_type=F32)
           + jnp.dot(mall, mid, preferred_element_type=F32)
           + jnp.dot(mall, lo, preferred_element_type=F32))
    pw = jnp.exp(cum)

    def blk(i):
        return pw[i * c:(i + 1) * c]

    level = level_ref[...]
    lane = lax.broadcasted_iota(jnp.int32, (1, LANES), 1)
    rowi = lax.broadcasted_iota(jnp.int32, (LANES, 1), 0)
    qb = q * blk(0)
    kf = k * blk(1)
    for pp in range(GLA_KWIDTH // LANES):
        sl = slice(pp * LANES, (pp + 1) * LANES)
        s_pair = s_scr[sl, :]
        kft = kf[:, sl].T.astype(BF16)
        decay = jnp.exp(jnp.sum(la[:, sl].T, axis=1, keepdims=True))
        s_new = decay * s_pair
        for hh in range(2):
            h = 2 * pp + hh
            hm = (lane < GLA_DK) if hh == 0 else (lane >= GLA_DK)
            vh = v[:, h * GLA_DV:(h + 1) * GLA_DV].astype(BF16)
            qm = jnp.where(hm, q[:, sl], 0.0)
            att = jnp.where(level == GLA_LEVELS + 1,
                            lax.dot_general(qm.astype(BF16), k[:, sl].astype(BF16), _NT,
                                            preferred_element_type=F32), 0.0)
            for l in range(1, GLA_LEVELS + 1):
                qd = (qm * blk(2 * l)[:, sl]).astype(BF16)
                ke = (k[:, sl] * blk(2 * l + 1)[:, sl]).astype(BF16)
                att = att + jnp.where(level == l,
                                      lax.dot_general(qd, ke, _NT, preferred_element_type=F32), 0.0)
            o = jnp.dot(att.astype(BF16), vh, preferred_element_type=F32)
            o = o + jnp.dot(jnp.where(hm, qb[:, sl], 0.0).astype(BF16), s_pair.astype(BF16),
                            preferred_element_type=F32)
            o = _rmsnorm(o, gn_ref[:, h * GLA_DV:(h + 1) * GLA_DV])
            rg = rows(r_ref[0][:, h * GLA_DV:(h + 1) * GLA_DV])
            o = o * (rg * jax.nn.sigmoid(rg))
            o_ref[0, :, h * GLA_DV:(h + 1) * GLA_DV] = o[:lb]
            rm = (rowi < GLA_DK) if hh == 0 else (rowi >= GLA_DK)
            s_new = s_new + jnp.where(rm, jnp.dot(kft, vh, preferred_element_type=F32), 0.0)
        s_scr[sl, :] = s_new

    @pl.when(ci == pl.num_programs(1) - 1)
    def _():
        sfin_ref[0] = s_scr[...]


def _gla(qk, v, r, la, s0, gn):
    b, l, _ = qk.shape
    c = GLA_CHUNK
    lb = c if l % c == 0 else l
    assert lb == c or (l < c and l % SUBLANES == 0)
    mall, level = _gla_constants()
    const = lambda bi, ci: (0, 0)
    tok = lambda bi, ci: (bi, ci, 0)
    seq = lambda bi, ci: (bi, 0, 0)
    return pl.pallas_call(
        _gla_kernel,
        grid=(b, l // lb),
        in_specs=[pl.BlockSpec(mall.shape, const),
                  pl.BlockSpec(level.shape, const),
                  pl.BlockSpec((1, lb, 2 * GLA_KWIDTH), tok),
                  pl.BlockSpec((1, lb, GLA_VWIDTH), tok),
                  pl.BlockSpec((1, lb, GLA_VWIDTH), tok),
                  pl.BlockSpec((1, lb, GLA_KWIDTH), tok),
                  pl.BlockSpec((1, GLA_KWIDTH, GLA_DV), seq),
                  pl.BlockSpec((1, GLA_VWIDTH), const)],
        out_specs=[pl.BlockSpec((1, lb, GLA_VWIDTH), tok),
                   pl.BlockSpec((1, GLA_KWIDTH, GLA_DV), seq)],
        out_shape=[jax.ShapeDtypeStruct((b, l, GLA_VWIDTH), F32),
                   jax.ShapeDtypeStruct((b, GLA_KWIDTH, GLA_DV), F32)],
        scratch_shapes=[pltpu.VMEM((GLA_KWIDTH, GLA_DV), F32)],
        compiler_params=_cparams(("parallel", "arbitrary")),
        name="gla",
    )(mall, level, qk, v, r, la, s0, gn)


def _merge_kernel(x_ref, oa_ref, ob_ref, gates_ref, wsb_ref, wgl_ref, wout_ref, g2_ref, wq_ref,
                  x1_ref, h2_ref, qry_ref):
    d = x_ref.shape[1]
    ua = jnp.dot(oa_ref[...].astype(BF16), wsb_ref[...], preferred_element_type=F32)
    ub = jnp.dot(ob_ref[...].astype(BF16), wgl_ref[...], preferred_element_type=F32)
    merged = jax.nn.sigmoid(gates_ref[:, :d]) * ua + jax.nn.sigmoid(gates_ref[:, d:]) * ub
    x1 = x_ref[...] + jnp.dot(merged.astype(BF16), wout_ref[...], preferred_element_type=F32)
    x1_ref[...] = x1
    h2 = _rmsnorm(x1, g2_ref[...])
    h2_ref[...] = h2
    qry_ref[...] = jnp.dot(h2.astype(BF16), wq_ref[...], preferred_element_type=F32)


def _merge(x2d, oa, ob, gates, wsb, wgl, wout, g2, wq, tm):
    n, d = x2d.shape
    nq = wq.shape[1]
    const = lambda i: (0, 0)
    row = lambda i: (i, 0)
    return pl.pallas_call(
        _merge_kernel,
        grid=(n // tm,),
        in_specs=[pl.BlockSpec((tm, d), row),
                  pl.BlockSpec((tm, SB_WIDTH), row),
                  pl.BlockSpec((tm, GLA_VWIDTH), row),
                  pl.BlockSpec((tm, 2 * d), row),
                  pl.BlockSpec(wsb.shape, const),
                  pl.BlockSpec(wgl.shape, const),
                  pl.BlockSpec(wout.shape, const),
                  pl.BlockSpec((1, d), const),
                  pl.BlockSpec(wq.shape, const)],
        out_specs=[pl.BlockSpec((tm, d), row), pl.BlockSpec((tm, d), row), pl.BlockSpec((tm, nq), row)],
        out_shape=[jax.ShapeDtypeStruct((n, d), F32), jax.ShapeDtypeStruct((n, d), F32),
                   jax.ShapeDtypeStruct((n, nq), F32)],
        compiler_params=_cparams(("parallel",)),
        name="merge",
    )(x2d, oa, ob, gates, wsb, wgl, wout, g2, wq)


def _peer_candidates():
    return [(a, b) for a in range(PEER_TOPK) for b in range(PEER_TOPK) if (a + 1) * (b + 1) <= PEER_TOPK]


def _route_kernel(qry_ref, keys_ref, s0_ref, s1_ref, stat_ref, top_scr):
    neg = -jnp.inf
    for h in range(PEER_HEADS):
        for p in range(2):
            hp = 2 * h + p
            qh = qry_ref[:, hp * PEER_KEY_HALF:(hp + 1) * PEER_KEY_HALF]
            s = lax.dot_general(keys_ref[hp], qh, _NT, preferred_element_type=F32,
                                precision=lax.Precision.HIGHEST)
            (s0_ref, s1_ref)[p][h] = s
            cur = s
            for r in range(PEER_TOPK):
                m = jnp.max(cur, axis=0, keepdims=True)
                top_scr[p, r, h:h + 1, :] = m
                cur = jnp.where(cur >= m, neg, cur)
    cands = [top_scr[0, a] + top_scr[1, b] for (a, b) in _peer_candidates()]
    cmax = top_scr[0, 0] + top_scr[1, 0]
    zsum = jnp.zeros_like(cmax)
    m = cmax
    for r in range(PEER_TOPK):
        m = functools.reduce(jnp.maximum, cands)
        zsum = zsum + jnp.exp(m - cmax)
        cands = [jnp.where(cd >= m, neg, cd) for cd in cands]
    stat_ref[0] = m
    stat_ref[1] = top_scr[0, 0]
    stat_ref[2] = top_scr[1, 0]
    stat_ref[3] = zsum


def _route(qry, keys, t):
    n = qry.shape[0]
    nhp = 2 * PEER_HEADS
    return pl.pallas_call(
        _route_kernel,
        grid=(n // t,),
        in_specs=[pl.BlockSpec((t, nhp * PEER_KEY_HALF), lambda i: (i, 0)),
                  pl.BlockSpec((nhp, PEER_N_KEYS, PEER_KEY_HALF), lambda i: (0, 0, 0))],
        out_specs=[pl.BlockSpec((PEER_HEADS, PEER_N_KEYS, t), lambda i: (0, 0, i)),
                   pl.BlockSpec((PEER_HEADS, PEER_N_KEYS, t), lambda i: (0, 0, i)),
                   pl.BlockSpec((4, PEER_HEADS, t), lambda i: (0, 0, i))],
        out_shape=[jax.ShapeDtypeStruct((PEER_HEADS, PEER_N_KEYS, n), F32),
                   jax.ShapeDtypeStruct((PEER_HEADS, PEER_N_KEYS, n), F32),
                   jax.ShapeDtypeStruct((4, PEER_HEADS, n), F32)],
        scratch_shapes=[pltpu.VMEM((2, PEER_TOPK, PEER_HEADS, t), F32)],
        compiler_params=_cparams(("parallel",)),
        name="peer_route",
    )(qry, keys)


def _gelu(x):
    return 0.5 * x * (1.0 + lax.erf(x * (2.0 ** -0.5)))


def _peer_kernel(h2_ref, x1_ref, s0_ref, s1_ref, stat_ref, u_ref, vt_ref, gf_ref, y_ref,
                 xt_scr, acc_scr, b_scr, g_scr, *, t):
    j = pl.program_id(1)
    nk = PEER_N_KEYS
    ni = s0_ref.shape[1]

    @pl.when(j == 0)
    def _():
        xt_scr[...] = h2_ref[...].T.astype(BF16)
        acc_scr[...] = jnp.zeros_like(acc_scr)
        for h in range(PEER_HEADS):
            b_scr[h] = jnp.exp(s1_ref[h] - stat_ref[2, h:h + 1, :]) / stat_ref[3, h:h + 1, :]

    act = jnp.dot(u_ref[...], xt_scr[...], preferred_element_type=F32)
    for ii in range(ni):
        for lt in range(t // LANES):
            ls = slice(lt * LANES, (lt + 1) * LANES)
            w = jnp.zeros((nk, LANES), F32)
            for h in range(PEER_HEADS):
                s0 = s0_ref[h, ii:ii + 1, ls]
                a0 = jnp.exp(s0 - stat_ref[1, h:h + 1, ls])
                tau = stat_ref[0, h:h + 1, ls]
                sel = (s0 + s1_ref[h, :, ls]) >= tau
                w = w + jnp.where(sel, a0 * b_scr[h, :, ls], 0.0)
            g_scr[ii * nk:(ii + 1) * nk, ls] = (_gelu(act[ii * nk:(ii + 1) * nk, ls]) * w).astype(BF16)
    acc_scr[...] += jnp.dot(vt_ref[...], g_scr[...], preferred_element_type=F32)

    @pl.when(j == pl.num_programs(1) - 1)
    def _():
        y_ref[...] = _rmsnorm(x1_ref[...] + acc_scr[...].T, gf_ref[...])


def _peer(h2, x1, s0, s1, stat, u_bf, vt_bf, gf, t):
    n, d = h2.shape
    ne = u_bf.shape[0]
    ni = SUBLANES
    e_tile = ni * PEER_N_KEYS
    return pl.pallas_call(
        functools.partial(_peer_kernel, t=t),
        grid=(n // t, ne // e_tile),
        in_specs=[pl.BlockSpec((t, d), lambda i, j: (i, 0)),
                  pl.BlockSpec((t, d), lambda i, j: (i, 0)),
                  pl.BlockSpec((PEER_HEADS, ni, t), lambda i, j: (0, j, i)),
                  pl.BlockSpec((PEER_HEADS, PEER_N_KEYS, t), lambda i, j: (0, 0, i)),
                  pl.BlockSpec((4, PEER_HEADS, t), lambda i, j: (0, 0, i)),
                  pl.BlockSpec((e_tile, d), lambda i, j: (j, 0)),
                  pl.BlockSpec((d, e_tile), lambda i, j: (0, j)),
                  pl.BlockSpec((1, d), lambda i, j: (0, 0))],
        out_specs=pl.BlockSpec((t, d), lambda i, j: (i, 0)),
        out_shape=jax.ShapeDtypeStruct((n, d), F32),
        scratch_shapes=[pltpu.VMEM((d, t), BF16),
                        pltpu.VMEM((d, t), F32),
                        pltpu.VMEM((PEER_HEADS, PEER_N_KEYS, t), F32),
                        pltpu.VMEM((e_tile, t), BF16)],
        compiler_params=_cparams(("parallel", "arbitrary")),
        name="peer_experts",
    )(h2, x1, s0, s1, stat, u_bf, vt_bf, gf)


def _token_tile(n, pref):
    t = pref
    while n % t:
        t //= 2
    return t


def _group(x, sb_fn, s0, wts):
    b, l, d = x.shape
    n = b * l
    x2d = x.reshape(n, d)
    qa, ka, va, qkb, vb, rb, la, gates = _in_proj(
        x2d, wts['norm_mix_g'], wts['w_main'], wts['w_lr'], wts['w2'], wts['b2'], _token_tile(n, 256))
    r3 = lambda a: a.reshape(b, l, a.shape[-1])
    oa = sb_fn(r3(qa), r3(ka), r3(va))
    ob, s_fin = _gla(r3(qkb), r3(vb), r3(rb), r3(la), s0, wts['gla_norm_g'])
    x1, h2, qry = _merge(x2d, oa.reshape(n, SB_WIDTH), ob.reshape(n, GLA_VWIDTH), gates,
                         wts['w_sb_up'], wts['w_gla_up'], wts['w_out'], wts['norm_ffn_g'],
                         wts['peer_w_q'], _token_tile(n, 256))
    s0, s1, stat = _route(qry, wts['peer_keys'], _token_tile(n, 256))
    y = _peer(h2, x1, s0, s1, stat, wts['peer_u'], wts['peer_vt'], wts['norm_out_g'],
              _token_tile(n, 512))
    return y.reshape(b, l, d), ka, va, s_fin


def kernel(x_prompt, x_sample, cache_sb_k, cache_sb_v, page_table, state_gla, norm_mix_g, w_in, sb_bias,
           gla_gate_w2, gla_gate_b, gla_norm_g, w_sb_up, w_gla_up, w_out, norm_ffn_g, peer_w_q, peer_keys,
           peer_u, peer_v, norm_final_g):
    depth = w_in.shape[0]
    assert depth == 1, "the final norm is fused into the last (only) layer"
    bp, lp, d = x_prompt.shape
    bs, lsm, _ = x_sample.shape
    xp, xs = x_prompt, x_sample
    outs = [[] for _ in range(6)]
    for l in range(depth):
        w = w_in[l]
        o_lr = 3 * SB_WIDTH + 2 * GLA_KWIDTH + 2 * GLA_VWIDTH
        wts = dict(
            norm_mix_g=norm_mix_g[l].reshape(1, d),
            w_main=jnp.concatenate([w[:, :o_lr], w[:, o_lr + GLA_GATE_RANK:]], axis=1).astype(BF16),
            w_lr=jnp.pad(w[:, o_lr:o_lr + GLA_GATE_RANK], ((0, 0), (0, LANES - GLA_GATE_RANK))).astype(BF16),
            w2=jnp.pad(gla_gate_w2[l], ((0, LANES - GLA_GATE_RANK), (0, 0))),
            b2=gla_gate_b[l].reshape(1, GLA_KWIDTH),
            gla_norm_g=gla_norm_g[l].reshape(1, GLA_VWIDTH),
            w_sb_up=w_sb_up[l].astype(BF16),
            w_gla_up=w_gla_up[l].astype(BF16),
            w_out=w_out[l].astype(BF16),
            norm_ffn_g=norm_ffn_g[l].reshape(1, d),
            peer_w_q=peer_w_q[l].astype(BF16),
            peer_keys=peer_keys[l].reshape(2 * PEER_HEADS, PEER_N_KEYS, PEER_KEY_HALF),
            peer_u=peer_u[l].astype(BF16),
            peer_vt=peer_v[l].T.astype(BF16),
            norm_out_g=norm_final_g.reshape(1, d),
        )
        bias = sb_bias[l]
        ck = cache_sb_k[l].reshape(cache_sb_k.shape[1], cache_sb_k.shape[2], SB_WIDTH)
        cv = cache_sb_v[l].reshape(cache_sb_v.shape[1], cache_sb_v.shape[2], SB_WIDTH)

        s0p = jnp.zeros((bp, GLA_KWIDTH, GLA_DV), F32)
        xp, kp, vp, sp = _group(xp, lambda q, k, v: _sb_prompt(q, k, v, bias), s0p, wts)
        s0s = state_gla[l].reshape(bs, GLA_KWIDTH, GLA_DV)
        xs, ks, vs, ss = _group(xs, lambda q, k, v: _sb_sample(q, k, v, bias, ck, cv, page_table), s0s, wts)

        outs[0].append(kp.reshape(bp, lp, SB_HEADS, SB_HEAD_DIM))
        outs[1].append(vp.reshape(bp, lp, SB_HEADS, SB_HEAD_DIM))
        outs[2].append(sp.reshape(bp, GLA_HEADS, GLA_DK, GLA_DV))
        outs[3].append(ks.reshape(bs, lsm, SB_HEADS, SB_HEAD_DIM))
        outs[4].append(vs.reshape(bs, lsm, SB_HEADS, SB_HEAD_DIM))
        outs[5].append(ss.reshape(bs, GLA_HEADS, GLA_DK, GLA_DV))
    return (xp, xs) + tuple(o[0][None] for o in outs)
```

```python
import functools

import numpy as np
import jax
import jax.numpy as jnp
from jax import lax
from jax.experimental import pallas as pl
from jax.experimental.pallas import tpu as pltpu

F32 = jnp.float32
BF16 = jnp.bfloat16

NORM_EPS = 1e-6
SB_HEADS = 8
SB_HEAD_DIM = 64
SB_WIDTH = SB_HEADS * SB_HEAD_DIM
GLA_HEADS = 4
GLA_DK = 64
GLA_DV = 128
GLA_KWIDTH = GLA_HEADS * GLA_DK
GLA_VWIDTH = GLA_HEADS * GLA_DV
GLA_GATE_RANK = 16
GLA_GATE_TEMP = 16.0
PEER_HEADS = 8
PEER_N_KEYS = 128
PEER_KEY_HALF = 128
PEER_TOPK = 16

LANES = 128
SUBLANES = 8
VMEM_LIMIT_BYTES = 56 * 1024 * 1024
GLA_CHUNK = 128
GLA_LEVELS = 7
SB_TILE = 256
SB_PAGES_PER_STEP = 8

_NT = (((1,), (1,)), ((), ()))
_TN = (((0,), (0,)), ((), ()))


def _cparams(sem):
    return pltpu.CompilerParams(dimension_semantics=sem, vmem_limit_bytes=VMEM_LIMIT_BYTES)


def _rmsnorm(x, g):
    return x * lax.rsqrt(jnp.mean(x * x, axis=-1, keepdims=True) + NORM_EPS) * g


def _neg_softplus(z):
    return -(jnp.maximum(z, 0.0) + jnp.log1p(jnp.exp(-jnp.abs(z))))


def _split2(x):
    hi = x.astype(BF16)
    lo = (x - hi.astype(F32)).astype(BF16)
    return hi, lo


def _split3(x):
    hi = x.astype(BF16)
    r = x - hi.astype(F32)
    mid = r.astype(BF16)
    lo = (r - mid.astype(F32)).astype(BF16)
    return hi, mid, lo


def _inproj_kernel(x_ref, g_ref, w_ref, wlr_ref, w2_ref, b2_ref,
                   qa_ref, ka_ref, va_ref, qkb_ref, vb_ref, rb_ref, la_ref, gates_ref):
    h = _rmsnorm(x_ref[...], g_ref[...]).astype(BF16)

    def proj(lo, hi):
        return jnp.dot(h, w_ref[:, lo:hi], preferred_element_type=F32)

    o = 0
    qa_ref[...] = proj(o, o + SB_WIDTH) * (SB_HEAD_DIM ** -0.5); o += SB_WIDTH
    ka_ref[...] = proj(o, o + SB_WIDTH); o += SB_WIDTH
    va_ref[...] = proj(o, o + SB_WIDTH); o += SB_WIDTH
    qkb_ref[:, :GLA_KWIDTH] = proj(o, o + GLA_KWIDTH) * (GLA_DK ** -0.5); o += GLA_KWIDTH
    qkb_ref[:, GLA_KWIDTH:] = proj(o, o + GLA_KWIDTH); o += GLA_KWIDTH
    vb_ref[...] = proj(o, o + GLA_VWIDTH); o += GLA_VWIDTH
    rb_ref[...] = proj(o, o + GLA_VWIDTH); o += GLA_VWIDTH
    d = gates_ref.shape[1]
    gates_ref[...] = proj(o, o + d)
    gk = jnp.dot(h, wlr_ref[...], preferred_element_type=F32)
    pre = jnp.dot(gk, w2_ref[...], preferred_element_type=F32,
                  precision=lax.Precision.HIGHEST) + b2_ref[...]
    la_ref[...] = _neg_softplus(-pre) * (1.0 / GLA_GATE_TEMP)


def _in_proj(x2d, g, w_main, w_lr, w2, b2, tm):
    n, d = x2d.shape
    wm = w_main.shape[1]
    const = lambda i: (0, 0)
    row = lambda i: (i, 0)
    widths = [SB_WIDTH, SB_WIDTH, SB_WIDTH, 2 * GLA_KWIDTH, GLA_VWIDTH, GLA_VWIDTH, GLA_KWIDTH, 2 * d]
    return pl.pallas_call(
        _inproj_kernel,
        grid=(n // tm,),
        in_specs=[pl.BlockSpec((tm, d), row),
                  pl.BlockSpec((1, d), const),
                  pl.BlockSpec((d, wm), const),
                  pl.BlockSpec((d, LANES), const),
                  pl.BlockSpec((LANES, GLA_KWIDTH), const),
                  pl.BlockSpec((1, GLA_KWIDTH), const)],
        out_specs=[pl.BlockSpec((tm, w), row) for w in widths],
        out_shape=[jax.ShapeDtypeStruct((n, w), F32) for w in widths],
        compiler_params=_cparams(("parallel",)),
        name="in_proj",
    )(x2d, g, w_main, w_lr, w2, b2)


def _sb_prompt_kernel(bias_ref, cmat_ref, q_ref, k_ref, v_ref, o_ref, *, t):
    p = pl.program_id(1)
    qi = pl.program_id(2)
    lane = lax.broadcasted_iota(jnp.int32, (1, LANES), 1)
    first = lane < SB_HEAD_DIM
    q = q_ref[0]
    qh = (jnp.where(first, q, 0.0).astype(BF16), jnp.where(first, 0.0, q).astype(BF16))
    bias = (bias_ref[2 * p], bias_ref[2 * p + 1])

    def block(j, carry, masked):
        acc, c0, c1 = carry
        off = pl.multiple_of(j * t, t)
        kt = k_ref[0, pl.ds(off, t), :].astype(BF16)
        vt = v_ref[0, pl.ds(off, t), :]
        vh = (jnp.where(first, vt, 0.0).astype(BF16), jnp.where(first, 0.0, vt).astype(BF16))
        if masked:
            valid = (lax.broadcasted_iota(jnp.int32, (t, t), 1) < lax.broadcasted_iota(jnp.int32, (t, t), 0))
        cs = [c0, c1]
        for hh in range(2):
            z = lax.dot_general(qh[hh], kt, _NT, preferred_element_type=F32) + bias[hh]
            ls = _neg_softplus(z)
            lsig = z + ls
            if masked:
                ls = jnp.where(valid, ls, 0.0)
            r = jnp.dot(ls.astype(BF16), cmat_ref[...], preferred_element_type=F32)
            w = jnp.exp(lsig + r[:, :t] + cs[hh])
            if masked:
                w = jnp.where(valid, w, 0.0)
            acc = acc + jnp.dot(w.astype(BF16), vh[hh], preferred_element_type=F32)
            cs[hh] = cs[hh] + r[:, t:]
        return acc, cs[0], cs[1]

    carry = (jnp.zeros((t, LANES), F32), jnp.zeros((t, t), F32), jnp.zeros((t, t), F32))
    carry = block(qi, carry, True)
    carry = lax.fori_loop(0, qi, lambda jj, c: block(qi - 1 - jj, c, False), carry)
    o_ref[0] = carry[0]


def _sb_prompt(q, k, v, bias):
    b, l, _ = q.shape
    t = min(SB_TILE, l)
    later = np.arange(t)[:, None] > np.arange(t)[None, :]
    cmat = jnp.asarray(np.concatenate([later, np.ones((t, t), bool)], axis=1), BF16)
    return pl.pallas_call(
        functools.partial(_sb_prompt_kernel, t=t),
        grid=(b, SB_WIDTH // LANES, l // t),
        in_specs=[pl.BlockSpec(memory_space=pltpu.SMEM),
                  pl.BlockSpec((t, 2 * t), lambda bi, p, qi: (0, 0)),
                  pl.BlockSpec((1, t, LANES), lambda bi, p, qi: (bi, qi, p)),
                  pl.BlockSpec((1, l, LANES), lambda bi, p, qi: (bi, 0, p)),
                  pl.BlockSpec((1, l, LANES), lambda bi, p, qi: (bi, 0, p))],
        out_specs=pl.BlockSpec((1, t, LANES), lambda bi, p, qi: (bi, qi, p)),
        out_shape=jax.ShapeDtypeStruct((b, l, SB_WIDTH), F32),
        compiler_params=_cparams(("parallel", "parallel", "arbitrary")),
        name="sb_prompt",
    )(bias, cmat, q, k, v)


def _sb_sample_kernel(pt_ref, brow_ref, qt_ref, kn_ref, vn_ref, *rest, npg, ln):
    del pt_ref
    kp_refs = rest[:npg]
    vp_refs = rest[npg:2 * npg]
    o_ref, acc_ref, carry_ref, ls_scr, lw_scr = rest[2 * npg:]
    g = pl.program_id(1)
    nh = SB_HEADS
    qt = qt_ref[0]
    brow = brow_ref[...]

    def same_head(rows):
        lane = lax.broadcasted_iota(jnp.int32, (rows, LANES), 1)
        row = lax.broadcasted_iota(jnp.int32, (rows, LANES), 0)
        return (row % nh) == (lane // ln)

    def page(kp, vp, ns, keep, vis):
        rows = ns * nh
        kmat = kp.reshape(rows, SB_HEAD_DIM).astype(BF16)
        vmat = vp.reshape(rows, SB_HEAD_DIM).astype(BF16)
        z = jnp.dot(kmat, qt, preferred_element_type=F32) + brow
        ls = _neg_softplus(z)
        if vis is not None:
            ls = jnp.where(vis, ls, 0.0)
        ls_scr[:rows, :] = ls
        lw_scr[:rows, :] = z + ls
        run = carry_ref[...]
        for s in reversed(range(ns)):
            sl = slice(s * nh, (s + 1) * nh)
            lw_scr[sl, :] = lw_scr[sl, :] + run
            run = run + ls_scr[sl, :]
        carry_ref[...] = run
        w = jnp.where(keep, jnp.exp(lw_scr[:rows, :]), 0.0).astype(BF16)
        acc_ref[...] += lax.dot_general(w, vmat, _TN, preferred_element_type=F32)

    @pl.when(g == 0)
    def _():
        acc_ref[...] = jnp.zeros_like(acc_ref)
        carry_ref[...] = jnp.zeros_like(carry_ref)
        rows = ln * nh
        vis = (lax.broadcasted_iota(jnp.int32, (rows, LANES), 0) // nh
               < lax.broadcasted_iota(jnp.int32, (rows, LANES), 1) % ln)
        page(kn_ref[0], vn_ref[0], ln, same_head(rows) & vis, vis)

    pg = kp_refs[0].shape[0]
    keep = same_head(pg * nh)
    for i in range(npg):
        page(kp_refs[i][...], vp_refs[i][...], pg, keep, None)

    @pl.when(g == pl.num_programs(1) - 1)
    def _():
        o_ref[0] = jnp.concatenate([acc_ref[h * ln:(h + 1) * ln, :] for h in range(nh)], axis=1)


def _sb_sample(q, k_new, v_new, bias, cache_k, cache_v, layer, page_table):
    bd, ln, _ = q.shape
    n_pages = page_table.shape[1]
    pg = cache_k.shape[2]
    npg = SB_PAGES_PER_STEP
    while n_pages % npg:
        npg //= 2
    assert SB_HEADS == SUBLANES and SB_HEADS * ln <= LANES
    q4 = q.reshape(bd, ln, SB_HEADS, SB_HEAD_DIM)
    qt = q4.transpose(0, 3, 2, 1).reshape(bd, SB_HEAD_DIM, SB_HEADS * ln)
    qt = jnp.pad(qt, ((0, 0), (0, 0), (0, LANES - SB_HEADS * ln))).astype(BF16)
    brow = jnp.pad(jnp.repeat(bias, ln), (0, LANES - SB_HEADS * ln)).reshape(1, LANES)
    kn4 = k_new.reshape(bd, ln, SB_HEADS, SB_HEAD_DIM)
    vn4 = v_new.reshape(bd, ln, SB_HEADS, SB_HEAD_DIM)

    def page_spec(i):
        return pl.BlockSpec((None, None, pg, SB_HEADS, SB_HEAD_DIM),
                            lambda b, g, pt, i=i: (layer, pt[b, n_pages - 1 - (g * npg + i)], 0, 0, 0))

    seq = lambda b, g, pt: (b, 0, 0)
    seq4 = lambda b, g, pt: (b, 0, 0, 0)
    rows = max(pg, ln) * SB_HEADS
    grid_spec = pltpu.PrefetchScalarGridSpec(
        num_scalar_prefetch=1,
        grid=(bd, n_pages // npg),
        in_specs=[pl.BlockSpec((1, LANES), lambda b, g, pt: (0, 0)),
                  pl.BlockSpec((1, SB_HEAD_DIM, LANES), seq),
                  pl.BlockSpec((1, ln, SB_HEADS, SB_HEAD_DIM), seq4),
                  pl.BlockSpec((1, ln, SB_HEADS, SB_HEAD_DIM), seq4)]
                 + [page_spec(i) for i in range(npg)] * 2,
        out_specs=pl.BlockSpec((1, ln, SB_WIDTH), seq),
        scratch_shapes=[pltpu.VMEM((LANES, SB_HEAD_DIM), F32), pltpu.VMEM((SB_HEADS, LANES), F32),
                        pltpu.VMEM((rows, LANES), F32), pltpu.VMEM((rows, LANES), F32)],
    )
    return pl.pallas_call(
        functools.partial(_sb_sample_kernel, npg=npg, ln=ln),
        grid_spec=grid_spec,
        out_shape=jax.ShapeDtypeStruct((bd, ln, SB_WIDTH), F32),
        compiler_params=_cparams(("parallel", "arbitrary")),
        name="sb_sample",
    )(page_table, brow, qt, kn4, vn4, *([cache_k] * npg), *([cache_v] * npg))


def _gla_constants():
    c = GLA_CHUNK
    t = np.arange(c)[:, None]
    u = np.arange(c)[None, :]
    blocks = [(u <= t), (u > t)]
    level = np.zeros((c, c), np.int32)
    for l in range(1, GLA_LEVELS + 1):
        m = c >> l
        r = (t // (2 * m)) * (2 * m) + m - 1
        upper = (t % (2 * m)) >= m
        blocks.append(upper & (u > r) & (u <= t))
        blocks.append((~upper) & (u > t) & (u <= r))
        tt, ss = np.arange(c)[:, None], np.arange(c)[None, :]
        same = (tt // (2 * m)) == (ss // (2 * m))
        level[same & ((tt % (2 * m)) >= m) & ((ss % (2 * m)) < m)] = l
    level[np.arange(c), np.arange(c)] = GLA_LEVELS + 1
    mall = np.concatenate([b.astype(np.float32) for b in blocks], axis=0)
    return jnp.asarray(mall, BF16), jnp.asarray(level)


def _gla_kernel(mall_ref, level_ref, qk_ref, v_ref, r_ref, la_ref, s0_ref, gn_ref,
                o_ref, sfin_ref, s_scr):
    c = GLA_CHUNK
    lb = qk_ref.shape[1]
    ci = pl.program_id(1)

    @pl.when(ci == 0)
    def _():
        s_scr[...] = s0_ref[0]

    def rows(x):
        if lb == c:
            return x
        return jnp.concatenate([x, jnp.zeros((c - lb, x.shape[1]), x.dtype)], axis=0)

    qk = rows(qk_ref[0])
    q = qk[:, :GLA_KWIDTH]
    k = qk[:, GLA_KWIDTH:]
    v = rows(v_ref[0])
    la = rows(la_ref[0])
    hi, mid, lo = _split3(la)
    mall = mall_ref[...]
    cum = (jnp.dot(mall, hi, preferred_element_type=F32)
           + jnp.dot(mall, mid, preferred_element_type=F32)
           + jnp.dot(mall, lo, preferred_element_type=F32))
    pw = jnp.exp(cum)

    def blk(i):
        return pw[i * c:(i + 1) * c]

    level = level_ref[...]
    lane = lax.broadcasted_iota(jnp.int32, (1, LANES), 1)
    rowi = lax.broadcasted_iota(jnp.int32, (LANES, 1), 0)
    qb = q * blk(0)
    kf = k * blk(1)
    for pp in range(GLA_KWIDTH // LANES):
        sl = slice(pp * LANES, (pp + 1) * LANES)
        s_pair = s_scr[sl, :]
        kft = kf[:, sl].T.astype(BF16)
        decay = jnp.exp(jnp.sum(la[:, sl].T, axis=1, keepdims=True))
        s_new = decay * s_pair
        for hh in range(2):
            h = 2 * pp + hh
            hm = (lane < GLA_DK) if hh == 0 else (lane >= GLA_DK)
            vh = v[:, h * GLA_DV:(h + 1) * GLA_DV].astype(BF16)
            qm = jnp.where(hm, q[:, sl], 0.0)
            att = jnp.where(level == GLA_LEVELS + 1,
                            lax.dot_general(qm.astype(BF16), k[:, sl].astype(BF16), _NT,
                                            preferred_element_type=F32), 0.0)
            for l in range(1, GLA_LEVELS + 1):
                qd = (qm * blk(2 * l)[:, sl]).astype(BF16)
                ke = (k[:, sl] * blk(2 * l + 1)[:, sl]).astype(BF16)
                att = att + jnp.where(level == l,
                                      lax.dot_general(qd, ke, _NT, preferred_element_type=F32), 0.0)
            o = jnp.dot(att.astype(BF16), vh, preferred_element_type=F32)
            o = o + jnp.dot(jnp.where(hm, qb[:, sl], 0.0).astype(BF16), s_pair.astype(BF16),
                            preferred_element_type=F32)
            o = _rmsnorm(o, gn_ref[:, h * GLA_DV:(h + 1) * GLA_DV])
            rg = rows(r_ref[0][:, h * GLA_DV:(h + 1) * GLA_DV])
            o = o * (rg * jax.nn.sigmoid(rg))
            o_ref[0, :, h * GLA_DV:(h + 1) * GLA_DV] = o[:lb]
            rm = (rowi < GLA_DK) if hh == 0 else (rowi >= GLA_DK)
            s_new = s_new + jnp.where(rm, jnp.dot(kft, vh, preferred_element_type=F32), 0.0)
        s_scr[sl, :] = s_new

    @pl.when(ci == pl.num_programs(1) - 1)
    def _():
        sfin_ref[0] = s_scr[...]


def _gla(qk, v, r, la, s0, gn):
    b, l, _ = qk.shape
    c = GLA_CHUNK
    lb = c if l % c == 0 else l
    assert lb == c or (l < c and l % SUBLANES == 0)
    mall, level = _gla_constants()
    const = lambda bi, ci: (0, 0)
    tok = lambda bi, ci: (bi, ci, 0)
    seq = lambda bi, ci: (bi, 0, 0)
    return pl.pallas_call(
        _gla_kernel,
        grid=(b, l // lb),
        in_specs=[pl.BlockSpec(mall.shape, const),
                  pl.BlockSpec(level.shape, const),
                  pl.BlockSpec((1, lb, 2 * GLA_KWIDTH), tok),
                  pl.BlockSpec((1, lb, GLA_VWIDTH), tok),
                  pl.BlockSpec((1, lb, GLA_VWIDTH), tok),
                  pl.BlockSpec((1, lb, GLA_KWIDTH), tok),
                  pl.BlockSpec((1, GLA_KWIDTH, GLA_DV), seq),
                  pl.BlockSpec((1, GLA_VWIDTH), const)],
        out_specs=[pl.BlockSpec((1, lb, GLA_VWIDTH), tok),
                   pl.BlockSpec((1, GLA_KWIDTH, GLA_DV), seq)],
        out_shape=[jax.ShapeDtypeStruct((b, l, GLA_VWIDTH), F32),
                   jax.ShapeDtypeStruct((b, GLA_KWIDTH, GLA_DV), F32)],
        scratch_shapes=[pltpu.VMEM((GLA_KWIDTH, GLA_DV), F32)],
        compiler_params=_cparams(("parallel", "arbitrary")),
        name="gla",
    )(mall, level, qk, v, r, la, s0, gn)


def _merge_kernel(x_ref, oa_ref, ob_ref, gates_ref, wsb_ref, wgl_ref, wout_ref, g2_ref, wq_ref,
                  x1_ref, h2_ref, qry_ref):
    d = x_ref.shape[1]
    ua = jnp.dot(oa_ref[...].astype(BF16), wsb_ref[...], preferred_element_type=F32)
    ub = jnp.dot(ob_ref[...].astype(BF16), wgl_ref[...], preferred_element_type=F32)
    merged = jax.nn.sigmoid(gates_ref[:, :d]) * ua + jax.nn.sigmoid(gates_ref[:, d:]) * ub
    x1 = x_ref[...] + jnp.dot(merged.astype(BF16), wout_ref[...], preferred_element_type=F32)
    x1_ref[...] = x1
    h2 = _rmsnorm(x1, g2_ref[...])
    h2_ref[...] = h2
    qry_ref[...] = jnp.dot(h2.astype(BF16), wq_ref[...], preferred_element_type=F32)


def _merge(x2d, oa, ob, gates, wsb, wgl, wout, g2, wq, tm):
    n, d = x2d.shape
    nq = wq.shape[1]
    const = lambda i: (0, 0)
    row = lambda i: (i, 0)
    return pl.pallas_call(
        _merge_kernel,
        grid=(n // tm,),
        in_specs=[pl.BlockSpec((tm, d), row),
                  pl.BlockSpec((tm, SB_WIDTH), row),
                  pl.BlockSpec((tm, GLA_VWIDTH), row),
                  pl.BlockSpec((tm, 2 * d), row),
                  pl.BlockSpec(wsb.shape, const),
                  pl.BlockSpec(wgl.shape, const),
                  pl.BlockSpec(wout.shape, const),
                  pl.BlockSpec((1, d), const),
                  pl.BlockSpec(wq.shape, const)],
        out_specs=[pl.BlockSpec((tm, d), row), pl.BlockSpec((tm, d), row), pl.BlockSpec((tm, nq), row)],
        out_shape=[jax.ShapeDtypeStruct((n, d), F32), jax.ShapeDtypeStruct((n, d), F32),
                   jax.ShapeDtypeStruct((n, nq), F32)],
        compiler_params=_cparams(("parallel",)),
        name="merge",
    )(x2d, oa, ob, gates, wsb, wgl, wout, g2, wq)


def _peer_candidates():
    return [(a, b) for a in range(PEER_TOPK) for b in range(PEER_TOPK) if (a + 1) * (b + 1) <= PEER_TOPK]


def _route_kernel(qry_ref, keys_ref, a0_ref, thr_ref, b1_ref, top_scr, rank_scr):
    neg = -jnp.inf
    for h in range(PEER_HEADS):
        for p in range(2):
            hp = 2 * h + p
            qh = qry_ref[:, hp * PEER_KEY_HALF:(hp + 1) * PEER_KEY_HALF]
            s = lax.dot_general(keys_ref[hp], qh, _NT, preferred_element_type=F32,
                                precision=lax.Precision.HIGHEST)
            cur = s
            rank = jnp.full(s.shape, float(PEER_TOPK), F32)
            for r in range(PEER_TOPK):
                m = jnp.max(cur, axis=0, keepdims=True)
                top_scr[p, r, h:h + 1, :] = m
                hit = cur >= m
                if p == 0:
                    rank = jnp.where(hit, float(r), rank)
                cur = jnp.where(hit, neg, cur)
            e = jnp.exp(s - top_scr[p, 0, h:h + 1, :])
            if p == 0:
                rank_scr[h] = rank
                a0_ref[h] = e
            else:
                b1_ref[h] = e
    pairs = _peer_candidates()
    cands = [top_scr[0, a] + top_scr[1, b] for (a, b) in pairs]
    cmax = cands[0]
    zsum = jnp.zeros_like(cmax)
    tau = cmax
    for r in range(PEER_TOPK):
        tau = functools.reduce(jnp.maximum, cands)
        zsum = zsum + jnp.exp(tau - cmax)
        cands = [jnp.where(cd >= tau, neg, cd) for cd in cands]
    inv_z = 1.0 / zsum
    thr_a = [jnp.full(cmax.shape, jnp.inf, F32) for _ in range(PEER_TOPK)]
    for (a, b) in pairs:
        e1b = jnp.exp(top_scr[1, b] - top_scr[1, 0])
        thr_a[a] = jnp.where(top_scr[0, a] + top_scr[1, b] >= tau, e1b, thr_a[a])
    for h in range(PEER_HEADS):
        rank = rank_scr[h]
        thr = jnp.full(rank.shape, jnp.inf, F32)
        for a in range(PEER_TOPK):
            thr = jnp.where(rank == float(a), thr_a[a][h:h + 1, :], thr)
        thr_ref[h] = thr
        a0_ref[h] = a0_ref[h] * inv_z[h:h + 1, :]


def _route(qry, keys, t):
    n = qry.shape[0]
    nhp = 2 * PEER_HEADS
    blk = pl.BlockSpec((PEER_HEADS, PEER_N_KEYS, t), lambda i: (0, 0, i))
    shp = jax.ShapeDtypeStruct((PEER_HEADS, PEER_N_KEYS, n), F32)
    return pl.pallas_call(
        _route_kernel,
        grid=(n // t,),
        in_specs=[pl.BlockSpec((t, nhp * PEER_KEY_HALF), lambda i: (i, 0)),
                  pl.BlockSpec((nhp, PEER_N_KEYS, PEER_KEY_HALF), lambda i: (0, 0, 0))],
        out_specs=[blk, blk, blk],
        out_shape=[shp, shp, shp],
        scratch_shapes=[pltpu.VMEM((2, PEER_TOPK, PEER_HEADS, t), F32),
                        pltpu.VMEM((PEER_HEADS, PEER_N_KEYS, t), F32)],
        compiler_params=_cparams(("parallel",)),
        name="peer_route",
    )(qry, keys)


def _gelu(x):
    return 0.5 * x * (1.0 + lax.erf(x * (2.0 ** -0.5)))


def _peer_kernel(h2_ref, x1_ref, a0_ref, thr_ref, b1_ref, u_ref, vt_ref, gf_ref, y_ref,
                 xt_scr, acc_scr, g_scr, *, t):
    j = pl.program_id(1)
    nk = PEER_N_KEYS
    ni = a0_ref.shape[1]

    @pl.when(j == 0)
    def _():
        xt_scr[...] = h2_ref[...].T.astype(BF16)
        acc_scr[...] = jnp.zeros_like(acc_scr)

    act = jnp.dot(u_ref[...], xt_scr[...], preferred_element_type=F32)
    for ii in range(ni):
        for lt in range(t // LANES):
            ls = slice(lt * LANES, (lt + 1) * LANES)
            w = jnp.zeros((nk, LANES), F32)
            for h in range(PEER_HEADS):
                b1 = b1_ref[h, :, ls]
                w = w + jnp.where(b1 >= thr_ref[h, ii:ii + 1, ls], a0_ref[h, ii:ii + 1, ls] * b1, 0.0)
            g_scr[ii * nk:(ii + 1) * nk, ls] = (_gelu(act[ii * nk:(ii + 1) * nk, ls]) * w).astype(BF16)
    acc_scr[...] += jnp.dot(vt_ref[...], g_scr[...], preferred_element_type=F32)

    @pl.when(j == pl.num_programs(1) - 1)
    def _():
        y_ref[...] = _rmsnorm(x1_ref[...] + acc_scr[...].T, gf_ref[...])


def _peer(h2, x1, a0, thr, b1, u_bf, vt_bf, gf, t):
    n, d = h2.shape
    ne = u_bf.shape[0]
    ni = 2 * SUBLANES
    e_tile = ni * PEER_N_KEYS
    half0 = pl.BlockSpec((PEER_HEADS, ni, t), lambda i, j: (0, j, i))
    half1 = pl.BlockSpec((PEER_HEADS, PEER_N_KEYS, t), lambda i, j: (0, 0, i))
    return pl.pallas_call(
        functools.partial(_peer_kernel, t=t),
        grid=(n // t, ne // e_tile),
        in_specs=[pl.BlockSpec((t, d), lambda i, j: (i, 0)),
                  pl.BlockSpec((t, d), lambda i, j: (i, 0)),
                  half0, half0, half1,
                  pl.BlockSpec((e_tile, d), lambda i, j: (j, 0)),
                  pl.BlockSpec((d, e_tile), lambda i, j: (0, j)),
                  pl.BlockSpec((1, d), lambda i, j: (0, 0))],
        out_specs=pl.BlockSpec((t, d), lambda i, j: (i, 0)),
        out_shape=jax.ShapeDtypeStruct((n, d), F32),
        scratch_shapes=[pltpu.VMEM((d, t), BF16),
                        pltpu.VMEM((d, t), F32),
                        pltpu.VMEM((e_tile, t), BF16)],
        compiler_params=_cparams(("parallel", "arbitrary")),
        name="peer_experts",
    )(h2, x1, a0, thr, b1, u_bf, vt_bf, gf)


def _token_tile(n, pref):
    t = pref
    while n % t:
        t //= 2
    return t


def _group(x, sb_fn, s0, wts):
    b, l, d = x.shape
    n = b * l
    x2d = x.reshape(n, d)
    qa, ka, va, qkb, vb, rb, la, gates = _in_proj(
        x2d, wts['norm_mix_g'], wts['w_main'], wts['w_lr'], wts['w2'], wts['b2'], _token_tile(n, 256))
    r3 = lambda a: a.reshape(b, l, a.shape[-1])
    oa = sb_fn(r3(qa), r3(ka), r3(va))
    ob, s_fin = _gla(r3(qkb), r3(vb), r3(rb), r3(la), s0, wts['gla_norm_g'])
    x1, h2, qry = _merge(x2d, oa.reshape(n, SB_WIDTH), ob.reshape(n, GLA_VWIDTH), gates,
                         wts['w_sb_up'], wts['w_gla_up'], wts['w_out'], wts['norm_ffn_g'],
                         wts['peer_w_q'], _token_tile(n, 256))
    a0, thr, b1 = _route(qry, wts['peer_keys'], _token_tile(n, 256))
    y = _peer(h2, x1, a0, thr, b1, wts['peer_u'], wts['peer_vt'], wts['norm_out_g'],
              _token_tile(n, 512))
    return y.reshape(b, l, d), ka, va, s_fin


def kernel(x_prompt, x_sample, cache_sb_k, cache_sb_v, page_table, state_gla, norm_mix_g, w_in, sb_bias,
           gla_gate_w2, gla_gate_b, gla_norm_g, w_sb_up, w_gla_up, w_out, norm_ffn_g, peer_w_q, peer_keys,
           peer_u, peer_v, norm_final_g):
    depth = w_in.shape[0]
    assert depth == 1, "the final norm is fused into the last (only) layer"
    bp, lp, d = x_prompt.shape
    bs, lsm, _ = x_sample.shape
    xp, xs = x_prompt, x_sample
    outs = [[] for _ in range(6)]
    for l in range(depth):
        w = w_in[l]
        o_lr = 3 * SB_WIDTH + 2 * GLA_KWIDTH + 2 * GLA_VWIDTH
        wts = dict(
            norm_mix_g=norm_mix_g[l].reshape(1, d),
            w_main=jnp.concatenate([w[:, :o_lr], w[:, o_lr + GLA_GATE_RANK:]], axis=1).astype(BF16),
            w_lr=jnp.pad(w[:, o_lr:o_lr + GLA_GATE_RANK], ((0, 0), (0, LANES - GLA_GATE_RANK))).astype(BF16),
            w2=jnp.pad(gla_gate_w2[l], ((0, LANES - GLA_GATE_RANK), (0, 0))),
            b2=gla_gate_b[l].reshape(1, GLA_KWIDTH),
            gla_norm_g=gla_norm_g[l].reshape(1, GLA_VWIDTH),
            w_sb_up=w_sb_up[l].astype(BF16),
            w_gla_up=w_gla_up[l].astype(BF16),
            w_out=w_out[l].astype(BF16),
            norm_ffn_g=norm_ffn_g[l].reshape(1, d),
            peer_w_q=peer_w_q[l].astype(BF16),
            peer_keys=peer_keys[l].reshape(2 * PEER_HEADS, PEER_N_KEYS, PEER_KEY_HALF),
            peer_u=peer_u[l].astype(BF16),
            peer_vt=peer_v[l].T.astype(BF16),
            norm_out_g=norm_final_g.reshape(1, d),
        )
        bias = sb_bias[l]

        s0p = jnp.zeros((bp, GLA_KWIDTH, GLA_DV), F32)
        xp, kp, vp, sp = _group(xp, lambda q, k, v: _sb_prompt(q, k, v, bias), s0p, wts)
        s0s = state_gla[l].reshape(bs, GLA_KWIDTH, GLA_DV)
        xs, ks, vs, ss = _group(
            xs, lambda q, k, v: _sb_sample(q, k, v, bias, cache_sb_k, cache_sb_v, l, page_table), s0s, wts)

        outs[0].append(kp.reshape(bp, lp, SB_HEADS, SB_HEAD_DIM))
        outs[1].append(vp.reshape(bp, lp, SB_HEADS, SB_HEAD_DIM))
        outs[2].append(sp.reshape(bp, GLA_HEADS, GLA_DK, GLA_DV))
        outs[3].append(ks.reshape(bs, lsm, SB_HEADS, SB_HEAD_DIM))
        outs[4].append(vs.reshape(bs, lsm, SB_HEADS, SB_HEAD_DIM))
        outs[5].append(ss.reshape(bs, GLA_HEADS, GLA_DK, GLA_DV))
    return (xp, xs) + tuple(o[0][None] for o in outs)
```

```python
import functools

import numpy as np
import jax
import jax.numpy as jnp
from jax import lax
from jax.experimental import pallas as pl
from jax.experimental.pallas import tpu as pltpu

F32 = jnp.float32
BF16 = jnp.bfloat16

NORM_EPS = 1e-6
SB_HEADS = 8
SB_HEAD_DIM = 64
SB_WIDTH = SB_HEADS * SB_HEAD_DIM
GLA_HEADS = 4
GLA_DK = 64
GLA_DV = 128
GLA_KWIDTH = GLA_HEADS * GLA_DK
GLA_VWIDTH = GLA_HEADS * GLA_DV
GLA_GATE_RANK = 16
GLA_GATE_TEMP = 16.0
PEER_HEADS = 8
PEER_N_KEYS = 128
PEER_KEY_HALF = 128
PEER_TOPK = 16

LANES = 128
SUBLANES = 8
VMEM_LIMIT_BYTES = 56 * 1024 * 1024
GLA_CHUNK = 128
GLA_LEVELS = 7
SB_TILE = 256
SB_QUERY_TILE = 512
SB_PAGES_PER_STEP = 8

_NT = (((1,), (1,)), ((), ()))
_TN = (((0,), (0,)), ((), ()))


def _cparams(sem):
    return pltpu.CompilerParams(dimension_semantics=sem, vmem_limit_bytes=VMEM_LIMIT_BYTES)


def _rmsnorm(x, g):
    return x * lax.rsqrt(jnp.mean(x * x, axis=-1, keepdims=True) + NORM_EPS) * g


def _neg_softplus(z):
    return -(jnp.maximum(z, 0.0) + jnp.log1p(jnp.exp(-jnp.abs(z))))


def _aligned(i, m):
    return i if isinstance(i, int) else pl.multiple_of(i, m)


def _split2(x):
    hi = x.astype(BF16)
    lo = (x - hi.astype(F32)).astype(BF16)
    return hi, lo


def _split3(x):
    hi = x.astype(BF16)
    r = x - hi.astype(F32)
    mid = r.astype(BF16)
    lo = (r - mid.astype(F32)).astype(BF16)
    return hi, mid, lo


def _inproj_kernel(x_ref, g_ref, w_ref, wlr_ref, w2_ref, b2_ref,
                   qa_ref, ka_ref, va_ref, qkb_ref, vb_ref, rb_ref, la_ref, gates_ref):
    h = _rmsnorm(x_ref[...], g_ref[...]).astype(BF16)

    def proj(lo, hi):
        return jnp.dot(h, w_ref[:, lo:hi], preferred_element_type=F32)

    o = 0
    qa_ref[...] = proj(o, o + SB_WIDTH) * (SB_HEAD_DIM ** -0.5); o += SB_WIDTH
    ka_ref[...] = proj(o, o + SB_WIDTH); o += SB_WIDTH
    va_ref[...] = proj(o, o + SB_WIDTH); o += SB_WIDTH
    qkb_ref[:, :GLA_KWIDTH] = proj(o, o + GLA_KWIDTH) * (GLA_DK ** -0.5); o += GLA_KWIDTH
    qkb_ref[:, GLA_KWIDTH:] = proj(o, o + GLA_KWIDTH); o += GLA_KWIDTH
    vb_ref[...] = proj(o, o + GLA_VWIDTH); o += GLA_VWIDTH
    rb_ref[...] = proj(o, o + GLA_VWIDTH); o += GLA_VWIDTH
    d = gates_ref.shape[1]
    gates_ref[...] = proj(o, o + d)
    gk = jnp.dot(h, wlr_ref[...], preferred_element_type=F32)
    pre = jnp.dot(gk, w2_ref[...], preferred_element_type=F32,
                  precision=lax.Precision.HIGHEST) + b2_ref[...]
    la_ref[...] = _neg_softplus(-pre) * (1.0 / GLA_GATE_TEMP)


def _in_proj(x2d, g, w_main, w_lr, w2, b2, tm):
    n, d = x2d.shape
    wm = w_main.shape[1]
    const = lambda i: (0, 0)
    row = lambda i: (i, 0)
    widths = [SB_WIDTH, SB_WIDTH, SB_WIDTH, 2 * GLA_KWIDTH, GLA_VWIDTH, GLA_VWIDTH, GLA_KWIDTH, 2 * d]
    return pl.pallas_call(
        _inproj_kernel,
        grid=(n // tm,),
        in_specs=[pl.BlockSpec((tm, d), row),
                  pl.BlockSpec((1, d), const),
                  pl.BlockSpec((d, wm), const),
                  pl.BlockSpec((d, LANES), const),
                  pl.BlockSpec((LANES, GLA_KWIDTH), const),
                  pl.BlockSpec((1, GLA_KWIDTH), const)],
        out_specs=[pl.BlockSpec((tm, w), row) for w in widths],
        out_shape=[jax.ShapeDtypeStruct((n, w), F32) for w in widths],
        compiler_params=_cparams(("parallel",)),
        name="in_proj",
    )(x2d, g, w_main, w_lr, w2, b2)


def _sb_prompt_kernel(bias_ref, cmat_ref, q_ref, k_ref, v_ref, o_ref, *, tq, t):
    p = pl.program_id(1)
    qi = pl.program_id(2)
    lane = lax.broadcasted_iota(jnp.int32, (1, LANES), 1)
    first = lane < SB_HEAD_DIM
    q = q_ref[0]
    qh = (jnp.where(first, q, 0.0).astype(BF16), jnp.where(first, 0.0, q).astype(BF16))
    bias = (bias_ref[2 * p], bias_ref[2 * p + 1])
    nd = tq // t

    def block(j, carry, d):
        masked = d is not None
        acc, c0, c1 = carry
        off = _aligned(j * t, t)
        kt = k_ref[0, pl.ds(off, t), :].astype(BF16)
        vt = v_ref[0, pl.ds(off, t), :]
        vh = (jnp.where(first, vt, 0.0).astype(BF16), jnp.where(first, 0.0, vt).astype(BF16))
        if masked:
            valid = (lax.broadcasted_iota(jnp.int32, (tq, t), 1) + d * t
                     < lax.broadcasted_iota(jnp.int32, (tq, t), 0))
        cs = [c0, c1]
        for hh in range(2):
            z = lax.dot_general(qh[hh], kt, _NT, preferred_element_type=F32) + bias[hh]
            ls = _neg_softplus(z)
            lsig = z + ls
            if masked:
                ls = jnp.where(valid, ls, 0.0)
            r = jnp.dot(ls.astype(BF16), cmat_ref[...], preferred_element_type=F32)
            w = jnp.exp(lsig + r[:, :t] + cs[hh])
            if masked:
                w = jnp.where(valid, w, 0.0)
            acc = acc + jnp.dot(w.astype(BF16), vh[hh], preferred_element_type=F32)
            cs[hh] = cs[hh] + r[:, t:]
        return acc, cs[0], cs[1]

    carry = (jnp.zeros((tq, LANES), F32), jnp.zeros((tq, t), F32), jnp.zeros((tq, t), F32))
    for d in reversed(range(nd)):
        carry = block(qi * nd + d, carry, d)
    carry = lax.fori_loop(0, qi * nd, lambda jj, c: block(qi * nd - 1 - jj, c, None), carry)
    o_ref[0] = carry[0]


def _sb_prompt(q, k, v, bias):
    b, l, _ = q.shape
    t = min(SB_TILE, l)
    tq = min(SB_QUERY_TILE, l)
    later = np.arange(t)[:, None] > np.arange(t)[None, :]
    cmat = jnp.asarray(np.concatenate([later, np.ones((t, t), bool)], axis=1), BF16)
    return pl.pallas_call(
        functools.partial(_sb_prompt_kernel, tq=tq, t=t),
        grid=(b, SB_WIDTH // LANES, l // tq),
        in_specs=[pl.BlockSpec(memory_space=pltpu.SMEM),
                  pl.BlockSpec((t, 2 * t), lambda bi, p, qi: (0, 0)),
                  pl.BlockSpec((1, tq, LANES), lambda bi, p, qi: (bi, qi, p)),
                  pl.BlockSpec((1, l, LANES), lambda bi, p, qi: (bi, 0, p)),
                  pl.BlockSpec((1, l, LANES), lambda bi, p, qi: (bi, 0, p))],
        out_specs=pl.BlockSpec((1, tq, LANES), lambda bi, p, qi: (bi, qi, p)),
        out_shape=jax.ShapeDtypeStruct((b, l, SB_WIDTH), F32),
        compiler_params=_cparams(("parallel", "parallel", "arbitrary")),
        name="sb_prompt",
    )(bias, cmat, q, k, v)


def _sb_sample_kernel(pt_ref, brow_ref, cmat_ref, qbd_ref, kn_ref, vn_ref, *rest, npg, ln):
    del pt_ref
    kp_refs = rest[:npg]
    vp_refs = rest[npg:2 * npg]
    o_ref, acc_ref, carry_ref = rest[2 * npg:]
    g = pl.program_id(1)
    pg = kp_refs[0].shape[3]
    qbd = qbd_ref[0]
    bcol = brow_ref[...]

    nr = SB_HEADS * ln

    def pages(kts, vts, valid):
        zs = [jnp.dot(qbd, kt.astype(BF16), preferred_element_type=F32) + bcol for kt in kts]
        lss = [_neg_softplus(z) for z in zs]
        lsigs = [z + ls for z, ls in zip(zs, lss)]
        if valid is not None:
            lss = [jnp.where(valid, ls, 0.0) for ls in lss]
        r = jnp.dot(jnp.concatenate(lss, axis=0).astype(BF16), cmat_ref[...], preferred_element_type=F32)
        run = carry_ref[...]
        acc = acc_ref[...]
        for i, (lsig, vt) in enumerate(zip(lsigs, vts)):
            w = jnp.exp(lsig + r[i * nr:(i + 1) * nr, :pg] + run)
            if valid is not None:
                w = jnp.where(valid, w, 0.0)
            acc = acc + lax.dot_general(w.astype(BF16), vt.astype(BF16), _NT, preferred_element_type=F32)
            run = run + r[i * nr:(i + 1) * nr, pg:]
        acc_ref[...] = acc
        carry_ref[...] = run

    @pl.when(g == 0)
    def _():
        acc_ref[...] = jnp.zeros_like(acc_ref)
        carry_ref[...] = jnp.zeros_like(carry_ref)
        s_lane = lax.broadcasted_iota(jnp.int32, (nr, pg), 1)
        t_row = lax.broadcasted_iota(jnp.int32, (nr, pg), 0) % ln
        pages([kn_ref[0]], [vn_ref[0]], s_lane < t_row)

    pages([r_[0, 0] for r_ in kp_refs], [r_[0, 0] for r_ in vp_refs], None)

    @pl.when(g == pl.num_programs(1) - 1)
    def _():
        lane = lax.broadcasted_iota(jnp.int32, (ln, SB_WIDTH), 1)
        out = jnp.zeros((ln, SB_WIDTH), F32)
        for h in range(SB_HEADS):
            out = out + jnp.where(lane // SB_HEAD_DIM == h, acc_ref[h * ln:(h + 1) * ln, :], 0.0)
        o_ref[0] = out


def _sb_sample(q, k_new, v_new, bias, cache_k, cache_v, layer, page_table):
    bd, ln, _ = q.shape
    n_pages = page_table.shape[1]
    pg = cache_k.shape[2]
    npg = SB_PAGES_PER_STEP
    while n_pages % npg:
        npg //= 2
    depth, n_pool = cache_k.shape[:2]
    nr = SB_HEADS * ln
    assert ln == SUBLANES and ln <= pg
    q4 = q.reshape(bd, ln, SB_HEADS, SB_HEAD_DIM)
    qbd = jnp.einsum('bthd,hg->bhtgd', q4, jnp.eye(SB_HEADS, dtype=F32)).reshape(bd, nr, SB_WIDTH).astype(BF16)
    bcol = jnp.broadcast_to(jnp.repeat(bias, ln)[:, None], (nr, pg))
    later = np.arange(pg)[:, None] > np.arange(pg)[None, :]
    cmat = jnp.asarray(np.concatenate([later, np.ones((pg, pg), bool)], axis=1), BF16)
    ckt = cache_k.transpose(0, 1, 3, 4, 2).reshape(depth, n_pool, SB_WIDTH, pg)
    cvt = cache_v.transpose(0, 1, 3, 4, 2).reshape(depth, n_pool, SB_WIDTH, pg)
    knt = jnp.pad(k_new.transpose(0, 2, 1), ((0, 0), (0, 0), (0, pg - ln)))
    vnt = jnp.pad(v_new.transpose(0, 2, 1), ((0, 0), (0, 0), (0, pg - ln)))

    def page_spec(i):
        return pl.BlockSpec((1, 1, SB_WIDTH, pg),
                            lambda b, g, pt, i=i: (layer, pt[b, n_pages - 1 - (g * npg + i)], 0, 0))

    seq = lambda b, g, pt: (b, 0, 0)
    const = lambda b, g, pt: (0, 0)
    grid_spec = pltpu.PrefetchScalarGridSpec(
        num_scalar_prefetch=1,
        grid=(bd, n_pages // npg),
        in_specs=[pl.BlockSpec((nr, pg), const),
                  pl.BlockSpec((pg, 2 * pg), const),
                  pl.BlockSpec((1, nr, SB_WIDTH), seq),
                  pl.BlockSpec((1, SB_WIDTH, pg), seq),
                  pl.BlockSpec((1, SB_WIDTH, pg), seq)]
                 + [page_spec(i) for i in range(npg)] * 2,
        out_specs=pl.BlockSpec((1, ln, SB_WIDTH), seq),
        scratch_shapes=[pltpu.VMEM((nr, SB_WIDTH), F32), pltpu.VMEM((nr, pg), F32)],
    )
    return pl.pallas_call(
        functools.partial(_sb_sample_kernel, npg=npg, ln=ln),
        grid_spec=grid_spec,
        out_shape=jax.ShapeDtypeStruct((bd, ln, SB_WIDTH), F32),
        compiler_params=_cparams(("parallel", "arbitrary")),
        name="sb_sample",
    )(page_table, bcol, cmat, qbd, knt, vnt, *([ckt] * npg), *([cvt] * npg))


def _gla_constants():
    c = GLA_CHUNK
    t = np.arange(c)[:, None]
    u = np.arange(c)[None, :]
    blocks = [(u <= t), (u > t)]
    level = np.zeros((c, c), np.int32)
    for l in range(1, GLA_LEVELS + 1):
        m = c >> l
        r = (t // (2 * m)) * (2 * m) + m - 1
        upper = (t % (2 * m)) >= m
        blocks.append(upper & (u > r) & (u <= t))
        blocks.append((~upper) & (u > t) & (u <= r))
        tt, ss = np.arange(c)[:, None], np.arange(c)[None, :]
        same = (tt // (2 * m)) == (ss // (2 * m))
        level[same & ((tt % (2 * m)) >= m) & ((ss % (2 * m)) < m)] = l
    level[np.arange(c), np.arange(c)] = GLA_LEVELS + 1
    mall = np.concatenate([b.astype(np.float32) for b in blocks], axis=0)
    return jnp.asarray(mall, BF16), jnp.asarray(level)


def _gla_kernel(mall_ref, level_ref, qk_ref, v_ref, r_ref, la_ref, s0_ref, gn_ref,
                o_ref, sfin_ref, s_scr):
    c = GLA_CHUNK
    lb = qk_ref.shape[1]
    ci = pl.program_id(1)

    @pl.when(ci == 0)
    def _():
        s_scr[...] = s0_ref[0]

    def rows(x):
        if lb == c:
            return x
        return jnp.concatenate([x, jnp.zeros((c - lb, x.shape[1]), x.dtype)], axis=0)

    qk = rows(qk_ref[0])
    q = qk[:, :GLA_KWIDTH]
    k = qk[:, GLA_KWIDTH:]
    v = rows(v_ref[0])
    la = rows(la_ref[0])
    hi, mid, lo = _split3(la)
    mall = mall_ref[...]
    cum = (jnp.dot(mall, hi, preferred_element_type=F32)
           + jnp.dot(mall, mid, preferred_element_type=F32)
           + jnp.dot(mall, lo, preferred_element_type=F32))
    pw = jnp.exp(cum)

    def blk(i):
        return pw[i * c:(i + 1) * c]

    level = level_ref[...]
    lane = lax.broadcasted_iota(jnp.int32, (1, LANES), 1)
    rowi = lax.broadcasted_iota(jnp.int32, (LANES, 1), 0)
    qb = q * blk(0)
    kf = k * blk(1)
    for pp in range(GLA_KWIDTH // LANES):
        sl = slice(pp * LANES, (pp + 1) * LANES)
        s_pair = s_scr[sl, :]
        kft = kf[:, sl].T.astype(BF16)
        decay = jnp.exp(jnp.sum(la[:, sl].T, axis=1, keepdims=True))
        s_new = decay * s_pair
        for hh in range(2):
            h = 2 * pp + hh
            hm = (lane < GLA_DK) if hh == 0 else (lane >= GLA_DK)
            vh = v[:, h * GLA_DV:(h + 1) * GLA_DV].astype(BF16)
            qm = jnp.where(hm, q[:, sl], 0.0)
            att = jnp.where(level == GLA_LEVELS + 1,
                            lax.dot_general(qm.astype(BF16), k[:, sl].astype(BF16), _NT,
                                            preferred_element_type=F32), 0.0)
            for l in range(1, GLA_LEVELS + 1):
                qd = (qm * blk(2 * l)[:, sl]).astype(BF16)
                ke = (k[:, sl] * blk(2 * l + 1)[:, sl]).astype(BF16)
                att = att + jnp.where(level == l,
                                      lax.dot_general(qd, ke, _NT, preferred_element_type=F32), 0.0)
            o = jnp.dot(att.astype(BF16), vh, preferred_element_type=F32)
            o = o + jnp.dot(jnp.where(hm, qb[:, sl], 0.0).astype(BF16), s_pair.astype(BF16),
                            preferred_element_type=F32)
            o = _rmsnorm(o, gn_ref[:, h * GLA_DV:(h + 1) * GLA_DV])
            rg = rows(r_ref[0][:, h * GLA_DV:(h + 1) * GLA_DV])
            o = o * (rg * jax.nn.sigmoid(rg))
            o_ref[0, :, h * GLA_DV:(h + 1) * GLA_DV] = o[:lb]
            rm = (rowi < GLA_DK) if hh == 0 else (rowi >= GLA_DK)
            s_new = s_new + jnp.where(rm, jnp.dot(kft, vh, preferred_element_type=F32), 0.0)
        s_scr[sl, :] = s_new

    @pl.when(ci == pl.num_programs(1) - 1)
    def _():
        sfin_ref[0] = s_scr[...]


def _gla(qk, v, r, la, s0, gn):
    b, l, _ = qk.shape
    c = GLA_CHUNK
    lb = c if l % c == 0 else l
    assert lb == c or (l < c and l % SUBLANES == 0)
    mall, level = _gla_constants()
    const = lambda bi, ci: (0, 0)
    tok = lambda bi, ci: (bi, ci, 0)
    seq = lambda bi, ci: (bi, 0, 0)
    return pl.pallas_call(
        _gla_kernel,
        grid=(b, l // lb),
        in_specs=[pl.BlockSpec(mall.shape, const),
                  pl.BlockSpec(level.shape, const),
                  pl.BlockSpec((1, lb, 2 * GLA_KWIDTH), tok),
                  pl.BlockSpec((1, lb, GLA_VWIDTH), tok),
                  pl.BlockSpec((1, lb, GLA_VWIDTH), tok),
                  pl.BlockSpec((1, lb, GLA_KWIDTH), tok),
                  pl.BlockSpec((1, GLA_KWIDTH, GLA_DV), seq),
                  pl.BlockSpec((1, GLA_VWIDTH), const)],
        out_specs=[pl.BlockSpec((1, lb, GLA_VWIDTH), tok),
                   pl.BlockSpec((1, GLA_KWIDTH, GLA_DV), seq)],
        out_shape=[jax.ShapeDtypeStruct((b, l, GLA_VWIDTH), F32),
                   jax.ShapeDtypeStruct((b, GLA_KWIDTH, GLA_DV), F32)],
        scratch_shapes=[pltpu.VMEM((GLA_KWIDTH, GLA_DV), F32)],
        compiler_params=_cparams(("parallel", "arbitrary")),
        name="gla",
    )(mall, level, qk, v, r, la, s0, gn)


def _merge_kernel(x_ref, oa_ref, ob_ref, gates_ref, wsb_ref, wgl_ref, wout_ref, g2_ref, wq_ref,
                  x1_ref, h2_ref, qry_ref):
    d = x_ref.shape[1]
    ua = jnp.dot(oa_ref[...].astype(BF16), wsb_ref[...], preferred_element_type=F32)
    ub = jnp.dot(ob_ref[...].astype(BF16), wgl_ref[...], preferred_element_type=F32)
    merged = jax.nn.sigmoid(gates_ref[:, :d]) * ua + jax.nn.sigmoid(gates_ref[:, d:]) * ub
    x1 = x_ref[...] + jnp.dot(merged.astype(BF16), wout_ref[...], preferred_element_type=F32)
    x1_ref[...] = x1
    h2 = _rmsnorm(x1, g2_ref[...])
    h2_ref[...] = h2
    qry_ref[...] = jnp.dot(h2.astype(BF16), wq_ref[...], preferred_element_type=F32)


def _merge(x2d, oa, ob, gates, wsb, wgl, wout, g2, wq, tm):
    n, d = x2d.shape
    nq = wq.shape[1]
    const = lambda i: (0, 0)
    row = lambda i: (i, 0)
    return pl.pallas_call(
        _merge_kernel,
        grid=(n // tm,),
        in_specs=[pl.BlockSpec((tm, d), row),
                  pl.BlockSpec((tm, SB_WIDTH), row),
                  pl.BlockSpec((tm, GLA_VWIDTH), row),
                  pl.BlockSpec((tm, 2 * d), row),
                  pl.BlockSpec(wsb.shape, const),
                  pl.BlockSpec(wgl.shape, const),
                  pl.BlockSpec(wout.shape, const),
                  pl.BlockSpec((1, d), const),
                  pl.BlockSpec(wq.shape, const)],
        out_specs=[pl.BlockSpec((tm, d), row), pl.BlockSpec((tm, d), row), pl.BlockSpec((tm, nq), row)],
        out_shape=[jax.ShapeDtypeStruct((n, d), F32), jax.ShapeDtypeStruct((n, d), F32),
                   jax.ShapeDtypeStruct((n, nq), F32)],
        compiler_params=_cparams(("parallel",)),
        name="merge",
    )(x2d, oa, ob, gates, wsb, wgl, wout, g2, wq)


def _peer_candidates():
    return [(a, b) for a in range(PEER_TOPK) for b in range(PEER_TOPK) if (a + 1) * (b + 1) <= PEER_TOPK]


def _route_kernel(qry_ref, keys_ref, a0_ref, thr_ref, b1_ref, top_scr, rank_scr):
    neg = -jnp.inf
    for h in range(PEER_HEADS):
        for p in range(2):
            hp = 2 * h + p
            qh = qry_ref[:, hp * PEER_KEY_HALF:(hp + 1) * PEER_KEY_HALF]
            s = lax.dot_general(keys_ref[hp], qh, _NT, preferred_element_type=F32,
                                precision=lax.Precision.HIGHEST)
            cur = s
            rank = jnp.full(s.shape, float(PEER_TOPK), F32)
            for r in range(PEER_TOPK):
                m = jnp.max(cur, axis=0, keepdims=True)
                top_scr[p, r, h:h + 1, :] = m
                hit = cur >= m
                if p == 0:
                    rank = jnp.where(hit, float(r), rank)
                cur = jnp.where(hit, neg, cur)
            e = jnp.exp(s - top_scr[p, 0, h:h + 1, :])
            if p == 0:
                rank_scr[h] = rank
                a0_ref[h] = e
            else:
                b1_ref[h] = e
    pairs = _peer_candidates()
    cands = [top_scr[0, a] + top_scr[1, b] for (a, b) in pairs]
    cmax = cands[0]
    zsum = jnp.zeros_like(cmax)
    tau = cmax
    for r in range(PEER_TOPK):
        tau = functools.reduce(jnp.maximum, cands)
        zsum = zsum + jnp.exp(tau - cmax)
        cands = [jnp.where(cd >= tau, neg, cd) for cd in cands]
    inv_z = 1.0 / zsum
    thr_a = [jnp.full(cmax.shape, jnp.inf, F32) for _ in range(PEER_TOPK)]
    for (a, b) in pairs:
        e1b = jnp.exp(top_scr[1, b] - top_scr[1, 0])
        thr_a[a] = jnp.where(top_scr[0, a] + top_scr[1, b] >= tau, e1b, thr_a[a])
    for h in range(PEER_HEADS):
        rank = rank_scr[h]
        thr = jnp.full(rank.shape, jnp.inf, F32)
        for a in range(PEER_TOPK):
            thr = jnp.where(rank == float(a), thr_a[a][h:h + 1, :], thr)
        thr_ref[h] = thr
        a0_ref[h] = a0_ref[h] * inv_z[h:h + 1, :]


def _route(qry, keys, t):
    n = qry.shape[0]
    nhp = 2 * PEER_HEADS
    blk = pl.BlockSpec((PEER_HEADS, PEER_N_KEYS, t), lambda i: (0, 0, i))
    shp = jax.ShapeDtypeStruct((PEER_HEADS, PEER_N_KEYS, n), F32)
    return pl.pallas_call(
        _route_kernel,
        grid=(n // t,),
        in_specs=[pl.BlockSpec((t, nhp * PEER_KEY_HALF), lambda i: (i, 0)),
                  pl.BlockSpec((nhp, PEER_N_KEYS, PEER_KEY_HALF), lambda i: (0, 0, 0))],
        out_specs=[blk, blk, blk],
        out_shape=[shp, shp, shp],
        scratch_shapes=[pltpu.VMEM((2, PEER_TOPK, PEER_HEADS, t), F32),
                        pltpu.VMEM((PEER_HEADS, PEER_N_KEYS, t), F32)],
        compiler_params=_cparams(("parallel",)),
        name="peer_route",
    )(qry, keys)


def _gelu(x):
    return 0.5 * x * (1.0 + lax.erf(x * (2.0 ** -0.5)))


def _peer_kernel(h2_ref, x1_ref, a0_ref, thr_ref, b1_ref, u_ref, vt_ref, gf_ref, y_ref,
                 xt_scr, acc_scr, g_scr, *, t):
    j = pl.program_id(1)
    nk = PEER_N_KEYS
    ni = a0_ref.shape[1]

    @pl.when(j == 0)
    def _():
        xt_scr[...] = h2_ref[...].T.astype(BF16)
        acc_scr[...] = jnp.zeros_like(acc_scr)

    act = jnp.dot(u_ref[...], xt_scr[...], preferred_element_type=F32)
    for ii in range(ni):
        for lt in range(t // LANES):
            ls = slice(lt * LANES, (lt + 1) * LANES)
            w = jnp.zeros((nk, LANES), F32)
            for h in range(PEER_HEADS):
                b1 = b1_ref[h, :, ls]
                w = w + jnp.where(b1 >= thr_ref[h, ii:ii + 1, ls], a0_ref[h, ii:ii + 1, ls] * b1, 0.0)
            g_scr[ii * nk:(ii + 1) * nk, ls] = (_gelu(act[ii * nk:(ii + 1) * nk, ls]) * w).astype(BF16)
    acc_scr[...] += jnp.dot(vt_ref[...], g_scr[...], preferred_element_type=F32)

    @pl.when(j == pl.num_programs(1) - 1)
    def _():
        y_ref[...] = _rmsnorm(x1_ref[...] + acc_scr[...].T, gf_ref[...])


def _peer(h2, x1, a0, thr, b1, u_bf, vt_bf, gf, t):
    n, d = h2.shape
    ne = u_bf.shape[0]
    ni = 2 * SUBLANES
    e_tile = ni * PEER_N_KEYS
    half0 = pl.BlockSpec((PEER_HEADS, ni, t), lambda i, j: (0, j, i))
    half1 = pl.BlockSpec((PEER_HEADS, PEER_N_KEYS, t), lambda i, j: (0, 0, i))
    return pl.pallas_call(
        functools.partial(_peer_kernel, t=t),
        grid=(n // t, ne // e_tile),
        in_specs=[pl.BlockSpec((t, d), lambda i, j: (i, 0)),
                  pl.BlockSpec((t, d), lambda i, j: (i, 0)),
                  half0, half0, half1,
                  pl.BlockSpec((e_tile, d), lambda i, j: (j, 0)),
                  pl.BlockSpec((d, e_tile), lambda i, j: (0, j)),
                  pl.BlockSpec((1, d), lambda i, j: (0, 0))],
        out_specs=pl.BlockSpec((t, d), lambda i, j: (i, 0)),
        out_shape=jax.ShapeDtypeStruct((n, d), F32),
        scratch_shapes=[pltpu.VMEM((d, t), BF16),
                        pltpu.VMEM((d, t), F32),
                        pltpu.VMEM((e_tile, t), BF16)],
        compiler_params=_cparams(("parallel", "arbitrary")),
        name="peer_experts",
    )(h2, x1, a0, thr, b1, u_bf, vt_bf, gf)


def _token_tile(n, pref):
    t = pref
    while n % t:
        t //= 2
    return t


def _group(x, sb_fn, s0, wts):
    b, l, d = x.shape
    n = b * l
    x2d = x.reshape(n, d)
    qa, ka, va, qkb, vb, rb, la, gates = _in_proj(
        x2d, wts['norm_mix_g'], wts['w_main'], wts['w_lr'], wts['w2'], wts['b2'], _token_tile(n, 256))
    r3 = lambda a: a.reshape(b, l, a.shape[-1])
    oa = sb_fn(r3(qa), r3(ka), r3(va))
    ob, s_fin = _gla(r3(qkb), r3(vb), r3(rb), r3(la), s0, wts['gla_norm_g'])
    x1, h2, qry = _merge(x2d, oa.reshape(n, SB_WIDTH), ob.reshape(n, GLA_VWIDTH), gates,
                         wts['w_sb_up'], wts['w_gla_up'], wts['w_out'], wts['norm_ffn_g'],
                         wts['peer_w_q'], _token_tile(n, 256))
    a0, thr, b1 = _route(qry, wts['peer_keys'], _token_tile(n, 256))
    y = _peer(h2, x1, a0, thr, b1, wts['peer_u'], wts['peer_vt'], wts['norm_out_g'],
              _token_tile(n, 512))
    return y.reshape(b, l, d), ka, va, s_fin


def kernel(x_prompt, x_sample, cache_sb_k, cache_sb_v, page_table, state_gla, norm_mix_g, w_in, sb_bias,
           gla_gate_w2, gla_gate_b, gla_norm_g, w_sb_up, w_gla_up, w_out, norm_ffn_g, peer_w_q, peer_keys,
           peer_u, peer_v, norm_final_g):
    depth = w_in.shape[0]
    assert depth == 1, "the final norm is fused into the last (only) layer"
    bp, lp, d = x_prompt.shape
    bs, lsm, _ = x_sample.shape
    xp, xs = x_prompt, x_sample
    outs = [[] for _ in range(6)]
    for l in range(depth):
        w = w_in[l]
        o_lr = 3 * SB_WIDTH + 2 * GLA_KWIDTH + 2 * GLA_VWIDTH
        wts = dict(
            norm_mix_g=norm_mix_g[l].reshape(1, d),
            w_main=jnp.concatenate([w[:, :o_lr], w[:, o_lr + GLA_GATE_RANK:]], axis=1).astype(BF16),
            w_lr=jnp.pad(w[:, o_lr:o_lr + GLA_GATE_RANK], ((0, 0), (0, LANES - GLA_GATE_RANK))).astype(BF16),
            w2=jnp.pad(gla_gate_w2[l], ((0, LANES - GLA_GATE_RANK), (0, 0))),
            b2=gla_gate_b[l].reshape(1, GLA_KWIDTH),
            gla_norm_g=gla_norm_g[l].reshape(1, GLA_VWIDTH),
            w_sb_up=w_sb_up[l].astype(BF16),
            w_gla_up=w_gla_up[l].astype(BF16),
            w_out=w_out[l].astype(BF16),
            norm_ffn_g=norm_ffn_g[l].reshape(1, d),
            peer_w_q=peer_w_q[l].astype(BF16),
            peer_keys=peer_keys[l].reshape(2 * PEER_HEADS, PEER_N_KEYS, PEER_KEY_HALF),
            peer_u=peer_u[l].astype(BF16),
            peer_vt=peer_v[l].T.astype(BF16),
            norm_out_g=norm_final_g.reshape(1, d),
        )
        bias = sb_bias[l]

        s0p = jnp.zeros((bp, GLA_KWIDTH, GLA_DV), F32)
        xp, kp, vp, sp = _group(xp, lambda q, k, v: _sb_prompt(q, k, v, bias), s0p, wts)
        s0s = state_gla[l].reshape(bs, GLA_KWIDTH, GLA_DV)
        xs, ks, vs, ss = _group(
            xs, lambda q, k, v: _sb_sample(q, k, v, bias, cache_sb_k, cache_sb_v, l, page_table), s0s, wts)

        outs[0].append(kp.reshape(bp, lp, SB_HEADS, SB_HEAD_DIM))
        outs[1].append(vp.reshape(bp, lp, SB_HEADS, SB_HEAD_DIM))
        outs[2].append(sp.reshape(bp, GLA_HEADS, GLA_DK, GLA_DV))
        outs[3].append(ks.reshape(bs, lsm, SB_HEADS, SB_HEAD_DIM))
        outs[4].append(vs.reshape(bs, lsm, SB_HEADS, SB_HEAD_DIM))
        outs[5].append(ss.reshape(bs, GLA_HEADS, GLA_DK, GLA_DV))
    return (xp, xs) + tuple(o[0][None] for o in outs)
```

```python
import functools

import numpy as np
import jax
import jax.numpy as jnp
from jax import lax
from jax.experimental import pallas as pl
from jax.experimental.pallas import tpu as pltpu

F32 = jnp.float32
BF16 = jnp.bfloat16

NORM_EPS = 1e-6
LOG2E = 1.4426950408889634
SB_HEADS = 8
SB_HEAD_DIM = 64
SB_WIDTH = SB_HEADS * SB_HEAD_DIM
GLA_HEADS = 4
GLA_DK = 64
GLA_DV = 128
GLA_KWIDTH = GLA_HEADS * GLA_DK
GLA_VWIDTH = GLA_HEADS * GLA_DV
GLA_GATE_RANK = 16
GLA_GATE_TEMP = 16.0
PEER_HEADS = 8
PEER_N_KEYS = 128
PEER_KEY_HALF = 128
PEER_TOPK = 16

LANES = 128
SUBLANES = 8
VMEM_LIMIT_BYTES = 56 * 1024 * 1024
GLA_CHUNK = 128
GLA_LEVELS = 7
GLA_SEQS_PER_STEP = 2
SB_TILE = 256
SB_QUERY_TILE = 512
SB_PAGES_PER_STEP = 8

_NT = (((1,), (1,)), ((), ()))
_TN = (((0,), (0,)), ((), ()))


def _cparams(sem):
    return pltpu.CompilerParams(dimension_semantics=sem, vmem_limit_bytes=VMEM_LIMIT_BYTES)


def _rmsnorm(x, g):
    return x * lax.rsqrt(jnp.mean(x * x, axis=-1, keepdims=True) + NORM_EPS) * g


def _neg_softplus(z):
    return -(jnp.maximum(z, 0.0) + jnp.log1p(jnp.exp(-jnp.abs(z))))


def _stick_log2(z2):
    mn = jnp.minimum(z2, 0.0)
    t = jnp.log2(1.0 + jnp.exp2(mn + mn - z2))
    lsig = mn - t
    return lsig, lsig - z2


def _aligned(i, m):
    return i if isinstance(i, int) else pl.multiple_of(i, m)


def _split2(x):
    hi = x.astype(BF16)
    lo = (x - hi.astype(F32)).astype(BF16)
    return hi, lo


def _split3(x):
    hi = x.astype(BF16)
    r = x - hi.astype(F32)
    mid = r.astype(BF16)
    lo = (r - mid.astype(F32)).astype(BF16)
    return hi, mid, lo


def _inproj_kernel(x_ref, g_ref, w_ref, wlr_ref, w2_ref, b2_ref,
                   qa_ref, ka_ref, va_ref, qkb_ref, vb_ref, rb_ref, la_ref, gates_ref):
    h = _rmsnorm(x_ref[...], g_ref[...]).astype(BF16)

    def proj(lo, hi):
        return jnp.dot(h, w_ref[:, lo:hi], preferred_element_type=F32)

    o = 0
    qa_ref[...] = proj(o, o + SB_WIDTH) * (SB_HEAD_DIM ** -0.5 * LOG2E); o += SB_WIDTH
    ka_ref[...] = proj(o, o + SB_WIDTH); o += SB_WIDTH
    va_ref[...] = proj(o, o + SB_WIDTH); o += SB_WIDTH
    qkb_ref[:, :GLA_KWIDTH] = proj(o, o + GLA_KWIDTH) * (GLA_DK ** -0.5); o += GLA_KWIDTH
    qkb_ref[:, GLA_KWIDTH:] = proj(o, o + GLA_KWIDTH); o += GLA_KWIDTH
    vb_ref[...] = proj(o, o + GLA_VWIDTH); o += GLA_VWIDTH
    rb_ref[...] = proj(o, o + GLA_VWIDTH); o += GLA_VWIDTH
    d = gates_ref.shape[1]
    gates_ref[...] = proj(o, o + d)
    gk = jnp.dot(h, wlr_ref[...], preferred_element_type=F32)
    pre = jnp.dot(gk, w2_ref[...], preferred_element_type=F32,
                  precision=lax.Precision.HIGHEST) + b2_ref[...]
    la_ref[...] = _neg_softplus(-pre) * (1.0 / GLA_GATE_TEMP)


def _in_proj(x2d, g, w_main, w_lr, w2, b2, tm):
    n, d = x2d.shape
    wm = w_main.shape[1]
    const = lambda i: (0, 0)
    row = lambda i: (i, 0)
    widths = [SB_WIDTH, SB_WIDTH, SB_WIDTH, 2 * GLA_KWIDTH, GLA_VWIDTH, GLA_VWIDTH, GLA_KWIDTH, 2 * d]
    return pl.pallas_call(
        _inproj_kernel,
        grid=(n // tm,),
        in_specs=[pl.BlockSpec((tm, d), row),
                  pl.BlockSpec((1, d), const),
                  pl.BlockSpec((d, wm), const),
                  pl.BlockSpec((d, LANES), const),
                  pl.BlockSpec((LANES, GLA_KWIDTH), const),
                  pl.BlockSpec((1, GLA_KWIDTH), const)],
        out_specs=[pl.BlockSpec((tm, w), row) for w in widths],
        out_shape=[jax.ShapeDtypeStruct((n, w), F32) for w in widths],
        compiler_params=_cparams(("parallel",)),
        name="in_proj",
    )(x2d, g, w_main, w_lr, w2, b2)


def _sb_prompt_kernel(bias_ref, cmat_ref, q_ref, k_ref, v_ref, o_ref, *, tq, t):
    p = pl.program_id(1)
    qi = pl.program_id(2)
    lane = lax.broadcasted_iota(jnp.int32, (1, LANES), 1)
    first = lane < SB_HEAD_DIM
    q = q_ref[0]
    qh = (jnp.where(first, q, 0.0).astype(BF16), jnp.where(first, 0.0, q).astype(BF16))
    bias = (bias_ref[2 * p], bias_ref[2 * p + 1])
    nd = tq // t

    def block(j, carry, d):
        masked = d is not None
        acc, c0, c1 = carry
        off = _aligned(j * t, t)
        kt = k_ref[0, pl.ds(off, t), :].astype(BF16)
        vt = v_ref[0, pl.ds(off, t), :]
        vh = (jnp.where(first, vt, 0.0).astype(BF16), jnp.where(first, 0.0, vt).astype(BF16))
        if masked:
            valid = (lax.broadcasted_iota(jnp.int32, (tq, t), 1) + d * t
                     < lax.broadcasted_iota(jnp.int32, (tq, t), 0))
        cs = [c0, c1]
        for hh in range(2):
            z = lax.dot_general(qh[hh], kt, _NT, preferred_element_type=F32) + bias[hh]
            lsig, ls = _stick_log2(z)
            if masked:
                ls = jnp.where(valid, ls, 0.0)
            r = jnp.dot(ls.astype(BF16), cmat_ref[...], preferred_element_type=F32)
            w = jnp.exp2(lsig + r[:, :t] + cs[hh])
            if masked:
                w = jnp.where(valid, w, 0.0)
            acc = acc + jnp.dot(w.astype(BF16), vh[hh], preferred_element_type=F32)
            cs[hh] = cs[hh] + r[:, t:]
        return acc, cs[0], cs[1]

    carry = (jnp.zeros((tq, LANES), F32), jnp.zeros((tq, t), F32), jnp.zeros((tq, t), F32))
    for d in reversed(range(nd)):
        carry = block(qi * nd + d, carry, d)
    carry = lax.fori_loop(0, qi * nd, lambda jj, c: block(qi * nd - 1 - jj, c, None), carry)
    o_ref[0] = carry[0]


def _sb_prompt(q, k, v, bias):
    b, l, _ = q.shape
    t = min(SB_TILE, l)
    tq = min(SB_QUERY_TILE, l)
    later = np.arange(t)[:, None] > np.arange(t)[None, :]
    cmat = jnp.asarray(np.concatenate([later, np.ones((t, t), bool)], axis=1), BF16)
    return pl.pallas_call(
        functools.partial(_sb_prompt_kernel, tq=tq, t=t),
        grid=(b, SB_WIDTH // LANES, l // tq),
        in_specs=[pl.BlockSpec(memory_space=pltpu.SMEM),
                  pl.BlockSpec((t, 2 * t), lambda bi, p, qi: (0, 0)),
                  pl.BlockSpec((1, tq, LANES), lambda bi, p, qi: (bi, qi, p)),
                  pl.BlockSpec((1, l, LANES), lambda bi, p, qi: (bi, 0, p)),
                  pl.BlockSpec((1, l, LANES), lambda bi, p, qi: (bi, 0, p))],
        out_specs=pl.BlockSpec((1, tq, LANES), lambda bi, p, qi: (bi, qi, p)),
        out_shape=jax.ShapeDtypeStruct((b, l, SB_WIDTH), F32),
        compiler_params=_cparams(("parallel", "parallel", "arbitrary")),
        name="sb_prompt",
    )(bias * LOG2E, cmat, q, k, v)


def _sb_sample_kernel(pt_ref, brow_ref, cmat_ref, qbd_ref, kn_ref, vn_ref, *rest, npg, ln):
    del pt_ref
    kp_refs = rest[:npg]
    vp_refs = rest[npg:2 * npg]
    o_ref, acc_ref, carry_ref = rest[2 * npg:]
    g = pl.program_id(1)
    pg = kp_refs[0].shape[3]
    qbd = qbd_ref[0]
    bcol = brow_ref[...]

    nr = SB_HEADS * ln

    def pages(kts, vts, valid):
        zs = [jnp.dot(qbd, kt.astype(BF16), preferred_element_type=F32) + bcol for kt in kts]
        lsigs, lss = zip(*[_stick_log2(z) for z in zs])
        if valid is not None:
            lss = [jnp.where(valid, ls, 0.0) for ls in lss]
        r = jnp.dot(jnp.concatenate(lss, axis=0).astype(BF16), cmat_ref[...], preferred_element_type=F32)
        run = carry_ref[...]
        acc = acc_ref[...]
        for i, (lsig, vt) in enumerate(zip(lsigs, vts)):
            w = jnp.exp2(lsig + r[i * nr:(i + 1) * nr, :pg] + run)
            if valid is not None:
                w = jnp.where(valid, w, 0.0)
            acc = acc + lax.dot_general(w.astype(BF16), vt.astype(BF16), _NT, preferred_element_type=F32)
            run = run + r[i * nr:(i + 1) * nr, pg:]
        acc_ref[...] = acc
        carry_ref[...] = run

    @pl.when(g == 0)
    def _():
        acc_ref[...] = jnp.zeros_like(acc_ref)
        carry_ref[...] = jnp.zeros_like(carry_ref)
        s_lane = lax.broadcasted_iota(jnp.int32, (nr, pg), 1)
        t_row = lax.broadcasted_iota(jnp.int32, (nr, pg), 0) % ln
        pages([kn_ref[0]], [vn_ref[0]], s_lane < t_row)

    pages([r_[0, 0] for r_ in kp_refs], [r_[0, 0] for r_ in vp_refs], None)

    @pl.when(g == pl.num_programs(1) - 1)
    def _():
        lane = lax.broadcasted_iota(jnp.int32, (ln, SB_WIDTH), 1)
        out = jnp.zeros((ln, SB_WIDTH), F32)
        for h in range(SB_HEADS):
            out = out + jnp.where(lane // SB_HEAD_DIM == h, acc_ref[h * ln:(h + 1) * ln, :], 0.0)
        o_ref[0] = out


def _sb_sample(q, k_new, v_new, bias, cache_k, cache_v, layer, page_table):
    bd, ln, _ = q.shape
    n_pages = page_table.shape[1]
    pg = cache_k.shape[2]
    npg = SB_PAGES_PER_STEP
    while n_pages % npg:
        npg //= 2
    depth, n_pool = cache_k.shape[:2]
    nr = SB_HEADS * ln
    assert ln == SUBLANES and ln <= pg
    q4 = q.reshape(bd, ln, SB_HEADS, SB_HEAD_DIM)
    qbd = jnp.einsum('bthd,hg->bhtgd', q4, jnp.eye(SB_HEADS, dtype=F32)).reshape(bd, nr, SB_WIDTH).astype(BF16)
    bcol = jnp.broadcast_to(jnp.repeat(bias * LOG2E, ln)[:, None], (nr, pg))
    later = np.arange(pg)[:, None] > np.arange(pg)[None, :]
    cmat = jnp.asarray(np.concatenate([later, np.ones((pg, pg), bool)], axis=1), BF16)
    ckt = cache_k.transpose(0, 1, 3, 4, 2).reshape(depth, n_pool, SB_WIDTH, pg)
    cvt = cache_v.transpose(0, 1, 3, 4, 2).reshape(depth, n_pool, SB_WIDTH, pg)
    knt = jnp.pad(k_new.transpose(0, 2, 1), ((0, 0), (0, 0), (0, pg - ln)))
    vnt = jnp.pad(v_new.transpose(0, 2, 1), ((0, 0), (0, 0), (0, pg - ln)))

    def page_spec(i):
        return pl.BlockSpec((1, 1, SB_WIDTH, pg),
                            lambda b, g, pt, i=i: (layer, pt[b, n_pages - 1 - (g * npg + i)], 0, 0))

    seq = lambda b, g, pt: (b, 0, 0)
    const = lambda b, g, pt: (0, 0)
    grid_spec = pltpu.PrefetchScalarGridSpec(
        num_scalar_prefetch=1,
        grid=(bd, n_pages // npg),
        in_specs=[pl.BlockSpec((nr, pg), const),
                  pl.BlockSpec((pg, 2 * pg), const),
                  pl.BlockSpec((1, nr, SB_WIDTH), seq),
                  pl.BlockSpec((1, SB_WIDTH, pg), seq),
                  pl.BlockSpec((1, SB_WIDTH, pg), seq)]
                 + [page_spec(i) for i in range(npg)] * 2,
        out_specs=pl.BlockSpec((1, ln, SB_WIDTH), seq),
        scratch_shapes=[pltpu.VMEM((nr, SB_WIDTH), F32), pltpu.VMEM((nr, pg), F32)],
    )
    return pl.pallas_call(
        functools.partial(_sb_sample_kernel, npg=npg, ln=ln),
        grid_spec=grid_spec,
        out_shape=jax.ShapeDtypeStruct((bd, ln, SB_WIDTH), F32),
        compiler_params=_cparams(("parallel", "arbitrary")),
        name="sb_sample",
    )(page_table, bcol, cmat, qbd, knt, vnt, *([ckt] * npg), *([cvt] * npg))


def _gla_constants():
    c = GLA_CHUNK
    t = np.arange(c)[:, None]
    u = np.arange(c)[None, :]
    blocks = [(u <= t), (u > t)]
    level = np.zeros((c, c), np.int32)
    for l in range(1, GLA_LEVELS + 1):
        m = c >> l
        r = (t // (2 * m)) * (2 * m) + m - 1
        upper = (t % (2 * m)) >= m
        blocks.append(upper & (u > r) & (u <= t))
        blocks.append((~upper) & (u > t) & (u <= r))
        tt, ss = np.arange(c)[:, None], np.arange(c)[None, :]
        same = (tt // (2 * m)) == (ss // (2 * m))
        level[same & ((tt % (2 * m)) >= m) & ((ss % (2 * m)) < m)] = l
    level[np.arange(c), np.arange(c)] = GLA_LEVELS + 1
    mall = np.concatenate([b.astype(np.float32) for b in blocks], axis=0)
    return jnp.asarray(mall, BF16), jnp.asarray(level)


def _gla_kernel(mall_ref, level_ref, qk_ref, v_ref, r_ref, la_ref, s0_ref, gn_ref,
                o_ref, sfin_ref, s_scr):
    ci = pl.program_id(1)

    @pl.when(ci == 0)
    def _():
        s_scr[...] = s0_ref[...]

    for bi in range(qk_ref.shape[0]):
        _gla_chunk(bi, mall_ref, level_ref, qk_ref, v_ref, r_ref, la_ref, gn_ref, o_ref, s_scr)

    @pl.when(ci == pl.num_programs(1) - 1)
    def _():
        sfin_ref[...] = s_scr[...]


def _gla_chunk(bi, mall_ref, level_ref, qk_ref, v_ref, r_ref, la_ref, gn_ref, o_ref, s_scr):
    c = GLA_CHUNK
    lb = qk_ref.shape[1]

    def rows(x):
        if lb == c:
            return x
        return jnp.concatenate([x, jnp.zeros((c - lb, x.shape[1]), x.dtype)], axis=0)

    qk = rows(qk_ref[bi])
    q = qk[:, :GLA_KWIDTH]
    k = qk[:, GLA_KWIDTH:]
    v = rows(v_ref[bi])
    la = rows(la_ref[bi])
    hi, mid, lo = _split3(la)
    mall = mall_ref[...]
    cum = (jnp.dot(mall, hi, preferred_element_type=F32)
           + jnp.dot(mall, mid, preferred_element_type=F32)
           + jnp.dot(mall, lo, preferred_element_type=F32))
    pw = jnp.exp(cum)

    def blk(i):
        return pw[i * c:(i + 1) * c]

    level = level_ref[...]
    lane = lax.broadcasted_iota(jnp.int32, (1, LANES), 1)
    rowi = lax.broadcasted_iota(jnp.int32, (LANES, 1), 0)
    qb = q * blk(0)
    kf = k * blk(1)
    for pp in range(GLA_KWIDTH // LANES):
        sl = slice(pp * LANES, (pp + 1) * LANES)
        s_pair = s_scr[bi, sl, :]
        kft = kf[:, sl].T.astype(BF16)
        decay = jnp.exp(jnp.sum(la[:, sl].T, axis=1, keepdims=True))
        s_new = decay * s_pair
        for hh in range(2):
            h = 2 * pp + hh
            hm = (lane < GLA_DK) if hh == 0 else (lane >= GLA_DK)
            vh = v[:, h * GLA_DV:(h + 1) * GLA_DV].astype(BF16)
            qm = jnp.where(hm, q[:, sl], 0.0)
            att = jnp.where(level == GLA_LEVELS + 1,
                            lax.dot_general(qm.astype(BF16), k[:, sl].astype(BF16), _NT,
                                            preferred_element_type=F32), 0.0)
            for l in range(1, GLA_LEVELS + 1):
                qd = (qm * blk(2 * l)[:, sl]).astype(BF16)
                ke = (k[:, sl] * blk(2 * l + 1)[:, sl]).astype(BF16)
                att = att + jnp.where(level == l,
                                      lax.dot_general(qd, ke, _NT, preferred_element_type=F32), 0.0)
            o = jnp.dot(att.astype(BF16), vh, preferred_element_type=F32)
            o = o + jnp.dot(jnp.where(hm, qb[:, sl], 0.0).astype(BF16), s_pair.astype(BF16),
                            preferred_element_type=F32)
            o = _rmsnorm(o, gn_ref[:, h * GLA_DV:(h + 1) * GLA_DV])
            rg = rows(r_ref[bi][:, h * GLA_DV:(h + 1) * GLA_DV])
            o = o * (rg * jax.nn.sigmoid(rg))
            o_ref[bi, :, h * GLA_DV:(h + 1) * GLA_DV] = o[:lb]
            rm = (rowi < GLA_DK) if hh == 0 else (rowi >= GLA_DK)
            s_new = s_new + jnp.where(rm, jnp.dot(kft, vh, preferred_element_type=F32), 0.0)
        s_scr[bi, sl, :] = s_new


def _gla(qk, v, r, la, s0, gn):
    b, l, _ = qk.shape
    c = GLA_CHUNK
    lb = c if l % c == 0 else l
    assert lb == c or (l < c and l % SUBLANES == 0)
    mall, level = _gla_constants()
    nb = GLA_SEQS_PER_STEP if b % GLA_SEQS_PER_STEP == 0 else 1
    const = lambda bi, ci: (0, 0)
    tok = lambda bi, ci: (bi, ci, 0)
    seq = lambda bi, ci: (bi, 0, 0)
    return pl.pallas_call(
        _gla_kernel,
        grid=(b // nb, l // lb),
        in_specs=[pl.BlockSpec(mall.shape, const),
                  pl.BlockSpec(level.shape, const),
                  pl.BlockSpec((nb, lb, 2 * GLA_KWIDTH), tok),
                  pl.BlockSpec((nb, lb, GLA_VWIDTH), tok),
                  pl.BlockSpec((nb, lb, GLA_VWIDTH), tok),
                  pl.BlockSpec((nb, lb, GLA_KWIDTH), tok),
                  pl.BlockSpec((nb, GLA_KWIDTH, GLA_DV), seq),
                  pl.BlockSpec((1, GLA_VWIDTH), const)],
        out_specs=[pl.BlockSpec((nb, lb, GLA_VWIDTH), tok),
                   pl.BlockSpec((nb, GLA_KWIDTH, GLA_DV), seq)],
        out_shape=[jax.ShapeDtypeStruct((b, l, GLA_VWIDTH), F32),
                   jax.ShapeDtypeStruct((b, GLA_KWIDTH, GLA_DV), F32)],
        scratch_shapes=[pltpu.VMEM((nb, GLA_KWIDTH, GLA_DV), F32)],
        compiler_params=_cparams(("parallel", "arbitrary")),
        name="gla",
    )(mall, level, qk, v, r, la, s0, gn)


def _merge_kernel(x_ref, oa_ref, ob_ref, gates_ref, wsb_ref, wgl_ref, wout_ref, g2_ref, wq_ref,
                  x1_ref, h2_ref, qry_ref):
    d = x_ref.shape[1]
    ua = jnp.dot(oa_ref[...].astype(BF16), wsb_ref[...], preferred_element_type=F32)
    ub = jnp.dot(ob_ref[...].astype(BF16), wgl_ref[...], preferred_element_type=F32)
    merged = jax.nn.sigmoid(gates_ref[:, :d]) * ua + jax.nn.sigmoid(gates_ref[:, d:]) * ub
    x1 = x_ref[...] + jnp.dot(merged.astype(BF16), wout_ref[...], preferred_element_type=F32)
    x1_ref[...] = x1
    h2 = _rmsnorm(x1, g2_ref[...])
    h2_ref[...] = h2
    qry_ref[...] = jnp.dot(h2.astype(BF16), wq_ref[...], preferred_element_type=F32)


def _merge(x2d, oa, ob, gates, wsb, wgl, wout, g2, wq, tm):
    n, d = x2d.shape
    nq = wq.shape[1]
    const = lambda i: (0, 0)
    row = lambda i: (i, 0)
    return pl.pallas_call(
        _merge_kernel,
        grid=(n // tm,),
        in_specs=[pl.BlockSpec((tm, d), row),
                  pl.BlockSpec((tm, SB_WIDTH), row),
                  pl.BlockSpec((tm, GLA_VWIDTH), row),
                  pl.BlockSpec((tm, 2 * d), row),
                  pl.BlockSpec(wsb.shape, const),
                  pl.BlockSpec(wgl.shape, const),
                  pl.BlockSpec(wout.shape, const),
                  pl.BlockSpec((1, d), const),
                  pl.BlockSpec(wq.shape, const)],
        out_specs=[pl.BlockSpec((tm, d), row), pl.BlockSpec((tm, d), row), pl.BlockSpec((tm, nq), row)],
        out_shape=[jax.ShapeDtypeStruct((n, d), F32), jax.ShapeDtypeStruct((n, d), F32),
                   jax.ShapeDtypeStruct((n, nq), F32)],
        compiler_params=_cparams(("parallel",)),
        name="merge",
    )(x2d, oa, ob, gates, wsb, wgl, wout, g2, wq)


def _peer_candidates():
    return [(a, b) for a in range(PEER_TOPK) for b in range(PEER_TOPK) if (a + 1) * (b + 1) <= PEER_TOPK]


def _route_kernel(qry_ref, keys_ref, a0_ref, c0_ref, r1_ref, b1_ref, top_scr, rank_scr):
    neg = -jnp.inf
    for h in range(PEER_HEADS):
        for p in range(2):
            hp = 2 * h + p
            qh = qry_ref[:, hp * PEER_KEY_HALF:(hp + 1) * PEER_KEY_HALF]
            s = lax.dot_general(keys_ref[hp], qh, _NT, preferred_element_type=F32,
                                precision=lax.Precision.HIGHEST)
            cur = s
            rank = jnp.full(s.shape, float(PEER_TOPK), F32)
            for r in range(PEER_TOPK):
                m = jnp.max(cur, axis=0, keepdims=True)
                top_scr[p, r, h:h + 1, :] = m
                hit = cur >= m
                rank = jnp.where(hit, float(r), rank)
                cur = jnp.where(hit, neg, cur)
            e = jnp.exp(s - top_scr[p, 0, h:h + 1, :])
            if p == 0:
                rank_scr[h] = rank
                a0_ref[h] = e
            else:
                r1_ref[h] = rank.astype(BF16)
                b1_ref[h] = e.astype(BF16)
    pairs = _peer_candidates()
    cands = [top_scr[0, a] + top_scr[1, b] for (a, b) in pairs]
    cmax = cands[0]
    zsum = jnp.zeros_like(cmax)
    tau = cmax
    for r in range(PEER_TOPK):
        tau = functools.reduce(jnp.maximum, cands)
        zsum = zsum + jnp.exp(tau - cmax)
        cands = [jnp.where(cd >= tau, neg, cd) for cd in cands]
    half_inv_z = 0.5 / zsum
    cnt = [jnp.zeros_like(cmax) for _ in range(PEER_TOPK)]
    for (a, b) in pairs:
        cnt[a] = cnt[a] + jnp.where(top_scr[0, a] + top_scr[1, b] >= tau, 1.0, 0.0)
    for h in range(PEER_HEADS):
        rank = rank_scr[h]
        c0 = jnp.zeros_like(rank)
        for a in range(PEER_TOPK):
            c0 = jnp.where(rank == float(a), cnt[a][h:h + 1, :], c0)
        c0_ref[h] = c0
        a0_ref[h] = a0_ref[h] * half_inv_z[h:h + 1, :]


def _route(qry, keys, t):
    n = qry.shape[0]
    nhp = 2 * PEER_HEADS
    blk = pl.BlockSpec((PEER_HEADS, PEER_N_KEYS, t), lambda i: (0, 0, i))
    shp = jax.ShapeDtypeStruct((PEER_HEADS, PEER_N_KEYS, n), F32)
    shp16 = jax.ShapeDtypeStruct((PEER_HEADS, PEER_N_KEYS, n), BF16)
    return pl.pallas_call(
        _route_kernel,
        grid=(n // t,),
        in_specs=[pl.BlockSpec((t, nhp * PEER_KEY_HALF), lambda i: (i, 0)),
                  pl.BlockSpec((nhp, PEER_N_KEYS, PEER_KEY_HALF), lambda i: (0, 0, 0))],
        out_specs=[blk, blk, blk, blk],
        out_shape=[shp, shp, shp16, shp16],
        scratch_shapes=[pltpu.VMEM((2, PEER_TOPK, PEER_HEADS, t), F32),
                        pltpu.VMEM((PEER_HEADS, PEER_N_KEYS, t), F32)],
        compiler_params=_cparams(("parallel",)),
        name="peer_route",
    )(qry, keys)


def _two_gelu(x):
    return x * (1.0 + lax.erf(x * (2.0 ** -0.5)))


def _row_tile_bf16(row, n):
    packed_rows = 2 * SUBLANES
    one = jnp.broadcast_to(row, (packed_rows, row.shape[1])).astype(BF16)
    return pltpu.repeat(one, n // packed_rows, axis=0)


def _peer_kernel(h2_ref, x1_ref, a0_ref, c0_ref, r1_ref, b1_ref, u_ref, vt_ref, gf_ref, y_ref,
                 xt_scr, acc_scr, g_scr, *, t):
    j = pl.program_id(1)
    nk = PEER_N_KEYS
    ni = a0_ref.shape[1]

    @pl.when(j == 0)
    def _():
        xt_scr[...] = h2_ref[...].T.astype(BF16)
        acc_scr[...] = jnp.zeros_like(acc_scr)

    act = jnp.dot(u_ref[...], xt_scr[...], preferred_element_type=F32)
    for ii in range(ni):
        for lt in range(t // LANES):
            ls = slice(lt * LANES, (lt + 1) * LANES)
            w = jnp.zeros((nk, LANES), BF16)
            for h in range(PEER_HEADS):
                a0 = _row_tile_bf16(a0_ref[h, ii:ii + 1, ls], nk)
                c0 = _row_tile_bf16(c0_ref[h, ii:ii + 1, ls], nk)
                w = w + (a0 * b1_ref[h, :, ls]) * jnp.clip(c0 - r1_ref[h, :, ls], 0.0, 1.0)
            g_scr[ii * nk:(ii + 1) * nk, ls] = _two_gelu(act[ii * nk:(ii + 1) * nk, ls]).astype(BF16) * w
    acc_scr[...] += jnp.dot(vt_ref[...], g_scr[...], preferred_element_type=F32)

    @pl.when(j == pl.num_programs(1) - 1)
    def _():
        y_ref[...] = _rmsnorm(x1_ref[...] + acc_scr[...].T, gf_ref[...])


def _peer(h2, x1, a0, c0, r1, b1, u_bf, vt_bf, gf, t):
    n, d = h2.shape
    ne = u_bf.shape[0]
    ni = 2 * SUBLANES
    e_tile = ni * PEER_N_KEYS
    half0 = pl.BlockSpec((PEER_HEADS, ni, t), lambda i, j: (0, j, i))
    half1 = pl.BlockSpec((PEER_HEADS, PEER_N_KEYS, t), lambda i, j: (0, 0, i))
    return pl.pallas_call(
        functools.partial(_peer_kernel, t=t),
        grid=(n // t, ne // e_tile),
        in_specs=[pl.BlockSpec((t, d), lambda i, j: (i, 0)),
                  pl.BlockSpec((t, d), lambda i, j: (i, 0)),
                  half0, half0, half1, half1,
                  pl.BlockSpec((e_tile, d), lambda i, j: (j, 0)),
                  pl.BlockSpec((d, e_tile), lambda i, j: (0, j)),
                  pl.BlockSpec((1, d), lambda i, j: (0, 0))],
        out_specs=pl.BlockSpec((t, d), lambda i, j: (i, 0)),
        out_shape=jax.ShapeDtypeStruct((n, d), F32),
        scratch_shapes=[pltpu.VMEM((d, t), BF16),
                        pltpu.VMEM((d, t), F32),
                        pltpu.VMEM((e_tile, t), BF16)],
        compiler_params=_cparams(("parallel", "arbitrary")),
        name="peer_experts",
    )(h2, x1, a0, c0, r1, b1, u_bf, vt_bf, gf)


def _token_tile(n, pref):
    t = pref
    while n % t:
        t //= 2
    return t


def _group(x, sb_fn, s0, wts):
    b, l, d = x.shape
    n = b * l
    x2d = x.reshape(n, d)
    qa, ka, va, qkb, vb, rb, la, gates = _in_proj(
        x2d, wts['norm_mix_g'], wts['w_main'], wts['w_lr'], wts['w2'], wts['b2'], _token_tile(n, 256))
    r3 = lambda a: a.reshape(b, l, a.shape[-1])
    oa = sb_fn(r3(qa), r3(ka), r3(va))
    ob, s_fin = _gla(r3(qkb), r3(vb), r3(rb), r3(la), s0, wts['gla_norm_g'])
    x1, h2, qry = _merge(x2d, oa.reshape(n, SB_WIDTH), ob.reshape(n, GLA_VWIDTH), gates,
                         wts['w_sb_up'], wts['w_gla_up'], wts['w_out'], wts['norm_ffn_g'],
                         wts['peer_w_q'], _token_tile(n, 256))
    a0, c0, r1, b1 = _route(qry, wts['peer_keys'], _token_tile(n, 256))
    y = _peer(h2, x1, a0, c0, r1, b1, wts['peer_u'], wts['peer_vt'], wts['norm_out_g'],
              _token_tile(n, 512))
    return y.reshape(b, l, d), ka, va, s_fin


def kernel(x_prompt, x_sample, cache_sb_k, cache_sb_v, page_table, state_gla, norm_mix_g, w_in, sb_bias,
           gla_gate_w2, gla_gate_b, gla_norm_g, w_sb_up, w_gla_up, w_out, norm_ffn_g, peer_w_q, peer_keys,
           peer_u, peer_v, norm_final_g):
    depth = w_in.shape[0]
    assert depth == 1, "the final norm is fused into the last (only) layer"
    bp, lp, d = x_prompt.shape
    bs, lsm, _ = x_sample.shape
    xp, xs = x_prompt, x_sample
    outs = [[] for _ in range(6)]
    for l in range(depth):
        w = w_in[l]
        o_lr = 3 * SB_WIDTH + 2 * GLA_KWIDTH + 2 * GLA_VWIDTH
        wts = dict(
            norm_mix_g=norm_mix_g[l].reshape(1, d),
            w_main=jnp.concatenate([w[:, :o_lr], w[:, o_lr + GLA_GATE_RANK:]], axis=1).astype(BF16),
            w_lr=jnp.pad(w[:, o_lr:o_lr + GLA_GATE_RANK], ((0, 0), (0, LANES - GLA_GATE_RANK))).astype(BF16),
            w2=jnp.pad(gla_gate_w2[l], ((0, LANES - GLA_GATE_RANK), (0, 0))),
            b2=gla_gate_b[l].reshape(1, GLA_KWIDTH),
            gla_norm_g=gla_norm_g[l].reshape(1, GLA_VWIDTH),
            w_sb_up=w_sb_up[l].astype(BF16),
            w_gla_up=w_gla_up[l].astype(BF16),
            w_out=w_out[l].astype(BF16),
            norm_ffn_g=norm_ffn_g[l].reshape(1, d),
            peer_w_q=peer_w_q[l].astype(BF16),
            peer_keys=peer_keys[l].reshape(2 * PEER_HEADS, PEER_N_KEYS, PEER_KEY_HALF),
            peer_u=peer_u[l].astype(BF16),
            peer_vt=peer_v[l].T.astype(BF16),
            norm_out_g=norm_final_g.reshape(1, d),
        )
        bias = sb_bias[l]

        s0p = jnp.zeros((bp, GLA_KWIDTH, GLA_DV), F32)
        xp, kp, vp, sp = _group(xp, lambda q, k, v: _sb_prompt(q, k, v, bias), s0p, wts)
        s0s = state_gla[l].reshape(bs, GLA_KWIDTH, GLA_DV)
        xs, ks, vs, ss = _group(
            xs, lambda q, k, v: _sb_sample(q, k, v, bias, cache_sb_k, cache_sb_v, l, page_table), s0s, wts)

        outs[0].append(kp.reshape(bp, lp, SB_HEADS, SB_HEAD_DIM))
        outs[1].append(vp.reshape(bp, lp, SB_HEADS, SB_HEAD_DIM))
        outs[2].append(sp.reshape(bp, GLA_HEADS, GLA_DK, GLA_DV))
        outs[3].append(ks.reshape(bs, lsm, SB_HEADS, SB_HEAD_DIM))
        outs[4].append(vs.reshape(bs, lsm, SB_HEADS, SB_HEAD_DIM))
        outs[5].append(ss.reshape(bs, GLA_HEADS, GLA_DK, GLA_DV))
    return (xp, xs) + tuple(o[0][None] for o in outs)
```

```python
import functools

import numpy as np
import jax
import jax.numpy as jnp
from jax import lax
from jax.experimental import pallas as pl
from jax.experimental.pallas import tpu as pltpu

F32 = jnp.float32
BF16 = jnp.bfloat16

NORM_EPS = 1e-6
LOG2E = 1.4426950408889634
SB_HEADS = 8
SB_HEAD_DIM = 64
SB_WIDTH = SB_HEADS * SB_HEAD_DIM
GLA_HEADS = 4
GLA_DK = 64
GLA_DV = 128
GLA_KWIDTH = GLA_HEADS * GLA_DK
GLA_VWIDTH = GLA_HEADS * GLA_DV
GLA_GATE_RANK = 16
GLA_GATE_TEMP = 16.0
PEER_HEADS = 8
PEER_N_KEYS = 128
PEER_KEY_HALF = 128
PEER_TOPK = 16

LANES = 128
SUBLANES = 8
VMEM_LIMIT_BYTES = 56 * 1024 * 1024
GLA_CHUNK = 128
GLA_LEVELS = 7
GLA_SEQS_PER_STEP = 2
SB_TILE = 256
SB_QUERY_TILE = 512
SB_PAGES_PER_STEP = 8

_NT = (((1,), (1,)), ((), ()))
_TN = (((0,), (0,)), ((), ()))


def _cparams(sem):
    return pltpu.CompilerParams(dimension_semantics=sem, vmem_limit_bytes=VMEM_LIMIT_BYTES)


def _rmsnorm(x, g):
    return x * lax.rsqrt(jnp.mean(x * x, axis=-1, keepdims=True) + NORM_EPS) * g


def _neg_softplus(z):
    return -(jnp.maximum(z, 0.0) + jnp.log1p(jnp.exp(-jnp.abs(z))))


def _stick_log2(z2):
    mn = jnp.minimum(z2, 0.0)
    t = jnp.log2(1.0 + jnp.exp2(mn + mn - z2))
    lsig = mn - t
    return lsig, lsig - z2


def _aligned(i, m):
    return i if isinstance(i, int) else pl.multiple_of(i, m)


def _split2(x):
    hi = x.astype(BF16)
    lo = (x - hi.astype(F32)).astype(BF16)
    return hi, lo


def _split3(x):
    hi = x.astype(BF16)
    r = x - hi.astype(F32)
    mid = r.astype(BF16)
    lo = (r - mid.astype(F32)).astype(BF16)
    return hi, mid, lo


def _inproj_kernel(x_ref, g_ref, w_ref, wkvt_ref, wlr_ref, w2_ref, b2_ref,
                   qa_ref, ka_ref, va_ref, qkb_ref, vb_ref, rb_ref, la_ref, gates_ref, *, kv_t):
    h = _rmsnorm(x_ref[...], g_ref[...]).astype(BF16)

    def proj(lo, hi):
        return jnp.dot(h, w_ref[:, lo:hi], preferred_element_type=F32)

    o = 0
    qa_ref[...] = proj(o, o + SB_WIDTH) * (SB_HEAD_DIM ** -0.5 * LOG2E); o += SB_WIDTH
    if kv_t:
        ka_ref[0] = lax.dot_general(wkvt_ref[:SB_WIDTH, :], h, _NT, preferred_element_type=F32)
        va_ref[0] = lax.dot_general(wkvt_ref[SB_WIDTH:, :], h, _NT, preferred_element_type=F32)
    else:
        ka_ref[...] = proj(o, o + SB_WIDTH)
        va_ref[...] = proj(o + SB_WIDTH, o + 2 * SB_WIDTH)
    o += 2 * SB_WIDTH
    qkb_ref[:, :GLA_KWIDTH] = proj(o, o + GLA_KWIDTH) * (GLA_DK ** -0.5); o += GLA_KWIDTH
    qkb_ref[:, GLA_KWIDTH:] = proj(o, o + GLA_KWIDTH); o += GLA_KWIDTH
    vb_ref[...] = proj(o, o + GLA_VWIDTH); o += GLA_VWIDTH
    rb_ref[...] = proj(o, o + GLA_VWIDTH); o += GLA_VWIDTH
    d = gates_ref.shape[1]
    gates_ref[...] = proj(o, o + d)
    gk = jnp.dot(h, wlr_ref[...], preferred_element_type=F32)
    pre = jnp.dot(gk, w2_ref[...], preferred_element_type=F32,
                  precision=lax.Precision.HIGHEST) + b2_ref[...]
    la_ref[...] = _neg_softplus(-pre) * (1.0 / GLA_GATE_TEMP)


def _in_proj(x2d, g, w_main, w_kvt, w_lr, w2, b2, tm, seq_len, kv_t):
    n, d = x2d.shape
    wm = w_main.shape[1]
    const = lambda i: (0, 0)
    row = lambda i: (i, 0)
    widths = [SB_WIDTH, SB_WIDTH, SB_WIDTH, 2 * GLA_KWIDTH, GLA_VWIDTH, GLA_VWIDTH, GLA_KWIDTH, 2 * d]
    out_specs = [pl.BlockSpec((tm, w), row) for w in widths]
    out_shape = [jax.ShapeDtypeStruct((n, w), F32) for w in widths]
    if kv_t:
        per_seq = seq_len // tm
        for i in (1, 2):
            out_specs[i] = pl.BlockSpec((1, SB_WIDTH, tm), lambda i: (i // per_seq, 0, i % per_seq))
            out_shape[i] = jax.ShapeDtypeStruct((n // seq_len, SB_WIDTH, seq_len), F32)
    return pl.pallas_call(
        functools.partial(_inproj_kernel, kv_t=kv_t),
        grid=(n // tm,),
        in_specs=[pl.BlockSpec((tm, d), row),
                  pl.BlockSpec((1, d), const),
                  pl.BlockSpec((d, wm), const),
                  pl.BlockSpec(w_kvt.shape, const),
                  pl.BlockSpec((d, LANES), const),
                  pl.BlockSpec((LANES, GLA_KWIDTH), const),
                  pl.BlockSpec((1, GLA_KWIDTH), const)],
        out_specs=out_specs,
        out_shape=out_shape,
        compiler_params=_cparams(("parallel",)),
        name="in_proj",
    )(x2d, g, w_main, w_kvt, w_lr, w2, b2)


def _sb_prompt_kernel(bias_ref, cmat_ref, q_ref, k_ref, v_ref, o_ref, *, tq, t):
    p = pl.program_id(1)
    qi = pl.program_id(2)
    lane = lax.broadcasted_iota(jnp.int32, (1, LANES), 1)
    first = lane < SB_HEAD_DIM
    first_row = lax.broadcasted_iota(jnp.int32, (LANES, 1), 0) < SB_HEAD_DIM
    q = q_ref[0]
    qh = (jnp.where(first, q, 0.0).astype(BF16), jnp.where(first, 0.0, q).astype(BF16))
    bias = (bias_ref[2 * p], bias_ref[2 * p + 1])
    nd = tq // t

    def block(j, carry, d):
        masked = d is not None
        acc, c0, c1 = carry
        off = _aligned(j * t, t)
        kt = k_ref[0, :, pl.ds(off, t)].astype(BF16)
        vt = v_ref[0, :, pl.ds(off, t)]
        vh = (jnp.where(first_row, vt, 0.0).astype(BF16), jnp.where(first_row, 0.0, vt).astype(BF16))
        if masked:
            valid = (lax.broadcasted_iota(jnp.int32, (tq, t), 1) + d * t
                     < lax.broadcasted_iota(jnp.int32, (tq, t), 0))
        cs = [c0, c1]
        for hh in range(2):
            z = jnp.dot(qh[hh], kt, preferred_element_type=F32) + bias[hh]
            lsig, ls = _stick_log2(z)
            if masked:
                ls = jnp.where(valid, ls, 0.0)
            r = jnp.dot(ls.astype(BF16), cmat_ref[...], preferred_element_type=F32)
            w = jnp.exp2(lsig + r[:, :t] + cs[hh])
            if masked:
                w = jnp.where(valid, w, 0.0)
            acc = acc + lax.dot_general(w.astype(BF16), vh[hh], _NT, preferred_element_type=F32)
            cs[hh] = cs[hh] + r[:, t:]
        return acc, cs[0], cs[1]

    carry = (jnp.zeros((tq, LANES), F32), jnp.zeros((tq, t), F32), jnp.zeros((tq, t), F32))
    for d in reversed(range(nd)):
        carry = block(qi * nd + d, carry, d)
    carry = lax.fori_loop(0, qi * nd, lambda jj, c: block(qi * nd - 1 - jj, c, None), carry)
    o_ref[0] = carry[0]


def _sb_prompt(q, k, v, bias):
    b, l, _ = q.shape
    t = min(SB_TILE, l)
    tq = min(SB_QUERY_TILE, l)
    later = np.arange(t)[:, None] > np.arange(t)[None, :]
    cmat = jnp.asarray(np.concatenate([later, np.ones((t, t), bool)], axis=1), BF16)
    return pl.pallas_call(
        functools.partial(_sb_prompt_kernel, tq=tq, t=t),
        grid=(b, SB_WIDTH // LANES, l // tq),
        in_specs=[pl.BlockSpec(memory_space=pltpu.SMEM),
                  pl.BlockSpec((t, 2 * t), lambda bi, p, qi: (0, 0)),
                  pl.BlockSpec((1, tq, LANES), lambda bi, p, qi: (bi, qi, p)),
                  pl.BlockSpec((1, LANES, l), lambda bi, p, qi: (bi, p, 0)),
                  pl.BlockSpec((1, LANES, l), lambda bi, p, qi: (bi, p, 0))],
        out_specs=pl.BlockSpec((1, tq, LANES), lambda bi, p, qi: (bi, qi, p)),
        out_shape=jax.ShapeDtypeStruct((b, l, SB_WIDTH), F32),
        compiler_params=_cparams(("parallel", "parallel", "arbitrary")),
        name="sb_prompt",
    )(bias * LOG2E, cmat, q, k, v)


def _sb_sample_kernel(pt_ref, brow_ref, cmat_ref, qbd_ref, kn_ref, vn_ref, *rest, npg, ln):
    del pt_ref
    kp_refs = rest[:npg]
    vp_refs = rest[npg:2 * npg]
    o_ref, acc_ref, carry_ref = rest[2 * npg:]
    g = pl.program_id(1)
    pg = kp_refs[0].shape[3]
    qbd = qbd_ref[0]
    bcol = brow_ref[...]

    nr = SB_HEADS * ln

    def pages(kts, vts, valid):
        zs = [jnp.dot(qbd, kt.astype(BF16), preferred_element_type=F32) + bcol for kt in kts]
        lsigs, lss = zip(*[_stick_log2(z) for z in zs])
        if valid is not None:
            lss = [jnp.where(valid, ls, 0.0) for ls in lss]
        r = jnp.dot(jnp.concatenate(lss, axis=0).astype(BF16), cmat_ref[...], preferred_element_type=F32)
        run = carry_ref[...]
        acc = acc_ref[...]
        for i, (lsig, vt) in enumerate(zip(lsigs, vts)):
            w = jnp.exp2(lsig + r[i * nr:(i + 1) * nr, :pg] + run)
            if valid is not None:
                w = jnp.where(valid, w, 0.0)
            acc = acc + lax.dot_general(w.astype(BF16), vt.astype(BF16), _NT, preferred_element_type=F32)
            run = run + r[i * nr:(i + 1) * nr, pg:]
        acc_ref[...] = acc
        carry_ref[...] = run

    @pl.when(g == 0)
    def _():
        acc_ref[...] = jnp.zeros_like(acc_ref)
        carry_ref[...] = jnp.zeros_like(carry_ref)
        s_lane = lax.broadcasted_iota(jnp.int32, (nr, pg), 1)
        t_row = lax.broadcasted_iota(jnp.int32, (nr, pg), 0) % ln
        pages([kn_ref[0]], [vn_ref[0]], s_lane < t_row)

    pages([r_[0, 0] for r_ in kp_refs], [r_[0, 0] for r_ in vp_refs], None)

    @pl.when(g == pl.num_programs(1) - 1)
    def _():
        lane = lax.broadcasted_iota(jnp.int32, (ln, SB_WIDTH), 1)
        out = jnp.zeros((ln, SB_WIDTH), F32)
        for h in range(SB_HEADS):
            out = out + jnp.where(lane // SB_HEAD_DIM == h, acc_ref[h * ln:(h + 1) * ln, :], 0.0)
        o_ref[0] = out


def _sb_sample(q, k_new, v_new, bias, cache_k, cache_v, layer, page_table):
    bd, ln, _ = q.shape
    n_pages = page_table.shape[1]
    pg = cache_k.shape[2]
    npg = SB_PAGES_PER_STEP
    while n_pages % npg:
        npg //= 2
    depth, n_pool = cache_k.shape[:2]
    nr = SB_HEADS * ln
    assert ln == SUBLANES and ln <= pg
    q4 = q.reshape(bd, ln, SB_HEADS, SB_HEAD_DIM)
    qbd = jnp.einsum('bthd,hg->bhtgd', q4, jnp.eye(SB_HEADS, dtype=F32)).reshape(bd, nr, SB_WIDTH).astype(BF16)
    bcol = jnp.broadcast_to(jnp.repeat(bias * LOG2E, ln)[:, None], (nr, pg))
    later = np.arange(pg)[:, None] > np.arange(pg)[None, :]
    cmat = jnp.asarray(np.concatenate([later, np.ones((pg, pg), bool)], axis=1), BF16)
    ckt = cache_k.transpose(0, 1, 3, 4, 2).reshape(depth, n_pool, SB_WIDTH, pg)
    cvt = cache_v.transpose(0, 1, 3, 4, 2).reshape(depth, n_pool, SB_WIDTH, pg)
    knt = jnp.pad(k_new.transpose(0, 2, 1), ((0, 0), (0, 0), (0, pg - ln)))
    vnt = jnp.pad(v_new.transpose(0, 2, 1), ((0, 0), (0, 0), (0, pg - ln)))

    def page_spec(i):
        return pl.BlockSpec((1, 1, SB_WIDTH, pg),
                            lambda b, g, pt, i=i: (layer, pt[b, n_pages - 1 - (g * npg + i)], 0, 0))

    seq = lambda b, g, pt: (b, 0, 0)
    const = lambda b, g, pt: (0, 0)
    grid_spec = pltpu.PrefetchScalarGridSpec(
        num_scalar_prefetch=1,
        grid=(bd, n_pages // npg),
        in_specs=[pl.BlockSpec((nr, pg), const),
                  pl.BlockSpec((pg, 2 * pg), const),
                  pl.BlockSpec((1, nr, SB_WIDTH), seq),
                  pl.BlockSpec((1, SB_WIDTH, pg), seq),
                  pl.BlockSpec((1, SB_WIDTH, pg), seq)]
                 + [page_spec(i) for i in range(npg)] * 2,
        out_specs=pl.BlockSpec((1, ln, SB_WIDTH), seq),
        scratch_shapes=[pltpu.VMEM((nr, SB_WIDTH), F32), pltpu.VMEM((nr, pg), F32)],
    )
    return pl.pallas_call(
        functools.partial(_sb_sample_kernel, npg=npg, ln=ln),
        grid_spec=grid_spec,
        out_shape=jax.ShapeDtypeStruct((bd, ln, SB_WIDTH), F32),
        compiler_params=_cparams(("parallel", "arbitrary")),
        name="sb_sample",
    )(page_table, bcol, cmat, qbd, knt, vnt, *([ckt] * npg), *([cvt] * npg))


def _gla_constants():
    c = GLA_CHUNK
    t = np.arange(c)[:, None]
    u = np.arange(c)[None, :]
    blocks = [(u <= t), (u > t)]
    level = np.zeros((c, c), np.int32)
    for l in range(1, GLA_LEVELS + 1):
        m = c >> l
        r = (t // (2 * m)) * (2 * m) + m - 1
        upper = (t % (2 * m)) >= m
        blocks.append(upper & (u > r) & (u <= t))
        blocks.append((~upper) & (u > t) & (u <= r))
        tt, ss = np.arange(c)[:, None], np.arange(c)[None, :]
        same = (tt // (2 * m)) == (ss // (2 * m))
        level[same & ((tt % (2 * m)) >= m) & ((ss % (2 * m)) < m)] = l
    level[np.arange(c), np.arange(c)] = GLA_LEVELS + 1
    mall = np.concatenate([b.astype(np.float32) for b in blocks], axis=0)
    return jnp.asarray(mall, BF16), jnp.asarray(level)


def _gla_kernel(mall_ref, level_ref, qk_ref, v_ref, r_ref, la_ref, s0_ref, gn_ref,
                o_ref, sfin_ref, s_scr):
    ci = pl.program_id(1)

    @pl.when(ci == 0)
    def _():
        s_scr[...] = s0_ref[...]

    for bi in range(qk_ref.shape[0]):
        _gla_chunk(bi, mall_ref, level_ref, qk_ref, v_ref, r_ref, la_ref, gn_ref, o_ref, s_scr)

    @pl.when(ci == pl.num_programs(1) - 1)
    def _():
        sfin_ref[...] = s_scr[...]


def _gla_chunk(bi, mall_ref, level_ref, qk_ref, v_ref, r_ref, la_ref, gn_ref, o_ref, s_scr):
    c = GLA_CHUNK
    lb = qk_ref.shape[1]

    def rows(x):
        if lb == c:
            return x
        return jnp.concatenate([x, jnp.zeros((c - lb, x.shape[1]), x.dtype)], axis=0)

    qk = rows(qk_ref[bi])
    q = qk[:, :GLA_KWIDTH]
    k = qk[:, GLA_KWIDTH:]
    v = rows(v_ref[bi])
    la = rows(la_ref[bi])
    hi, mid, lo = _split3(la)
    mall = mall_ref[...]
    cum = (jnp.dot(mall, hi, preferred_element_type=F32)
           + jnp.dot(mall, mid, preferred_element_type=F32)
           + jnp.dot(mall, lo, preferred_element_type=F32))
    pw = jnp.exp(cum)

    def blk(i):
        return pw[i * c:(i + 1) * c]

    level = level_ref[...]
    lane = lax.broadcasted_iota(jnp.int32, (1, LANES), 1)
    rowi = lax.broadcasted_iota(jnp.int32, (LANES, 1), 0)
    qb = q * blk(0)
    kf = k * blk(1)
    for pp in range(GLA_KWIDTH // LANES):
        sl = slice(pp * LANES, (pp + 1) * LANES)
        s_pair = s_scr[bi, sl, :]
        kft = kf[:, sl].T.astype(BF16)
        decay = jnp.exp(jnp.sum(la[:, sl].T, axis=1, keepdims=True))
        s_new = decay * s_pair
        hms = (lane < GLA_DK, lane >= GLA_DK)
        qm2 = jnp.concatenate([jnp.where(hm, q[:, sl], 0.0) for hm in hms], axis=0)
        level2 = jnp.concatenate([level, level], axis=0)
        att2 = jnp.where(level2 == GLA_LEVELS + 1,
                         lax.dot_general(qm2.astype(BF16), k[:, sl].astype(BF16), _NT,
                                         preferred_element_type=F32), 0.0)
        for l in range(1, GLA_LEVELS + 1):
            pd = blk(2 * l)[:, sl]
            qd = (qm2 * jnp.concatenate([pd, pd], axis=0)).astype(BF16)
            ke = (k[:, sl] * blk(2 * l + 1)[:, sl]).astype(BF16)
            att2 = att2 + jnp.where(level2 == l,
                                    lax.dot_general(qd, ke, _NT, preferred_element_type=F32), 0.0)
        for hh in range(2):
            h = 2 * pp + hh
            hm = hms[hh]
            vh = v[:, h * GLA_DV:(h + 1) * GLA_DV].astype(BF16)
            att = att2[hh * c:(hh + 1) * c]
            o = jnp.dot(att.astype(BF16), vh, preferred_element_type=F32)
            o = o + jnp.dot(jnp.where(hm, qb[:, sl], 0.0).astype(BF16), s_pair.astype(BF16),
                            preferred_element_type=F32)
            o = _rmsnorm(o, gn_ref[:, h * GLA_DV:(h + 1) * GLA_DV])
            rg = rows(r_ref[bi][:, h * GLA_DV:(h + 1) * GLA_DV])
            o = o * (rg * jax.nn.sigmoid(rg))
            o_ref[bi, :, h * GLA_DV:(h + 1) * GLA_DV] = o[:lb]
            rm = (rowi < GLA_DK) if hh == 0 else (rowi >= GLA_DK)
            s_new = s_new + jnp.where(rm, jnp.dot(kft, vh, preferred_element_type=F32), 0.0)
        s_scr[bi, sl, :] = s_new


def _gla(qk, v, r, la, s0, gn):
    b, l, _ = qk.shape
    c = GLA_CHUNK
    lb = c if l % c == 0 else l
    assert lb == c or (l < c and l % SUBLANES == 0)
    mall, level = _gla_constants()
    nb = GLA_SEQS_PER_STEP if b % GLA_SEQS_PER_STEP == 0 else 1
    const = lambda bi, ci: (0, 0)
    tok = lambda bi, ci: (bi, ci, 0)
    seq = lambda bi, ci: (bi, 0, 0)
    return pl.pallas_call(
        _gla_kernel,
        grid=(b // nb, l // lb),
        in_specs=[pl.BlockSpec(mall.shape, const),
                  pl.BlockSpec(level.shape, const),
                  pl.BlockSpec((nb, lb, 2 * GLA_KWIDTH), tok),
                  pl.BlockSpec((nb, lb, GLA_VWIDTH), tok),
                  pl.BlockSpec((nb, lb, GLA_VWIDTH), tok),
                  pl.BlockSpec((nb, lb, GLA_KWIDTH), tok),
                  pl.BlockSpec((nb, GLA_KWIDTH, GLA_DV), seq),
                  pl.BlockSpec((1, GLA_VWIDTH), const)],
        out_specs=[pl.BlockSpec((nb, lb, GLA_VWIDTH), tok),
                   pl.BlockSpec((nb, GLA_KWIDTH, GLA_DV), seq)],
        out_shape=[jax.ShapeDtypeStruct((b, l, GLA_VWIDTH), F32),
                   jax.ShapeDtypeStruct((b, GLA_KWIDTH, GLA_DV), F32)],
        scratch_shapes=[pltpu.VMEM((nb, GLA_KWIDTH, GLA_DV), F32)],
        compiler_params=_cparams(("parallel", "arbitrary")),
        name="gla",
    )(mall, level, qk, v, r, la, s0, gn)


def _merge_kernel(x_ref, oa_ref, ob_ref, gates_ref, wsb_ref, wgl_ref, wout_ref, g2_ref, wq_ref,
                  x1_ref, h2_ref, qry_ref):
    d = x_ref.shape[1]
    ua = jnp.dot(oa_ref[...].astype(BF16), wsb_ref[...], preferred_element_type=F32)
    ub = jnp.dot(ob_ref[...].astype(BF16), wgl_ref[...], preferred_element_type=F32)
    merged = jax.nn.sigmoid(gates_ref[:, :d]) * ua + jax.nn.sigmoid(gates_ref[:, d:]) * ub
    x1 = x_ref[...] + jnp.dot(merged.astype(BF16), wout_ref[...], preferred_element_type=F32)
    x1_ref[...] = x1
    h2 = _rmsnorm(x1, g2_ref[...])
    h2_ref[...] = h2
    qry_ref[...] = jnp.dot(h2.astype(BF16), wq_ref[...], preferred_element_type=F32)


def _merge(x2d, oa, ob, gates, wsb, wgl, wout, g2, wq, tm):
    n, d = x2d.shape
    nq = wq.shape[1]
    const = lambda i: (0, 0)
    row = lambda i: (i, 0)
    return pl.pallas_call(
        _merge_kernel,
        grid=(n // tm,),
        in_specs=[pl.BlockSpec((tm, d), row),
                  pl.BlockSpec((tm, SB_WIDTH), row),
                  pl.BlockSpec((tm, GLA_VWIDTH), row),
                  pl.BlockSpec((tm, 2 * d), row),
                  pl.BlockSpec(wsb.shape, const),
                  pl.BlockSpec(wgl.shape, const),
                  pl.BlockSpec(wout.shape, const),
                  pl.BlockSpec((1, d), const),
                  pl.BlockSpec(wq.shape, const)],
        out_specs=[pl.BlockSpec((tm, d), row), pl.BlockSpec((tm, d), row), pl.BlockSpec((tm, nq), row)],
        out_shape=[jax.ShapeDtypeStruct((n, d), F32), jax.ShapeDtypeStruct((n, d), F32),
                   jax.ShapeDtypeStruct((n, nq), F32)],
        compiler_params=_cparams(("parallel",)),
        name="merge",
    )(x2d, oa, ob, gates, wsb, wgl, wout, g2, wq)


RANK_CODE_BASE = 2.0 ** 126
RANK_CODE_STEP = 2.0 ** 106


def _peer_candidates():
    return [(a, b) for a in range(PEER_TOPK) for b in range(PEER_TOPK) if (a + 1) * (b + 1) <= PEER_TOPK]


def _route_kernel(qry_ref, keys_ref, a0_ref, c0_ref, r1_ref, b1_ref, top_scr, rank_scr):
    neg = -jnp.inf
    for h in range(PEER_HEADS):
        for p in range(2):
            hp = 2 * h + p
            qh = qry_ref[:, hp * PEER_KEY_HALF:(hp + 1) * PEER_KEY_HALF]
            s = lax.dot_general(keys_ref[hp], qh, _NT, preferred_element_type=F32,
                                precision=lax.Precision.HIGHEST)
            cur = s
            for r in range(PEER_TOPK):
                m = jnp.max(cur, axis=0, keepdims=True)
                top_scr[p, r, h:h + 1, :] = m
                cur = jnp.where(cur >= m, -(RANK_CODE_BASE + r * RANK_CODE_STEP), cur)
            rank = jnp.where(cur <= -RANK_CODE_BASE, (-cur - RANK_CODE_BASE) * (1.0 / RANK_CODE_STEP),
                             float(PEER_TOPK))
            e = jnp.exp(s - top_scr[p, 0, h:h + 1, :])
            if p == 0:
                rank_scr[h] = rank
                a0_ref[h] = e
            else:
                r1_ref[h] = rank.astype(BF16)
                b1_ref[h] = e.astype(BF16)
    pairs = _peer_candidates()
    cands = [top_scr[0, a] + top_scr[1, b] for (a, b) in pairs]
    cmax = cands[0]
    zsum = jnp.zeros_like(cmax)
    tau = cmax
    for r in range(PEER_TOPK):
        tau = functools.reduce(jnp.maximum, cands)
        zsum = zsum + jnp.exp(tau - cmax)
        cands = [jnp.where(cd >= tau, neg, cd) for cd in cands]
    half_inv_z = 0.5 / zsum
    cnt = [jnp.zeros_like(cmax) for _ in range(PEER_TOPK)]
    for (a, b) in pairs:
        cnt[a] = cnt[a] + jnp.where(top_scr[0, a] + top_scr[1, b] >= tau, 1.0, 0.0)
    for h in range(PEER_HEADS):
        rank = rank_scr[h]
        c0 = jnp.zeros_like(rank)
        for a in range(PEER_TOPK):
            c0 = jnp.where(rank == float(a), cnt[a][h:h + 1, :], c0)
        c0_ref[h] = c0
        a0_ref[h] = a0_ref[h] * half_inv_z[h:h + 1, :]


def _route(qry, keys, t):
    n = qry.shape[0]
    nhp = 2 * PEER_HEADS
    blk = pl.BlockSpec((PEER_HEADS, PEER_N_KEYS, t), lambda i: (0, 0, i))
    shp = jax.ShapeDtypeStruct((PEER_HEADS, PEER_N_KEYS, n), F32)
    shp16 = jax.ShapeDtypeStruct((PEER_HEADS, PEER_N_KEYS, n), BF16)
    return pl.pallas_call(
        _route_kernel,
        grid=(n // t,),
        in_specs=[pl.BlockSpec((t, nhp * PEER_KEY_HALF), lambda i: (i, 0)),
                  pl.BlockSpec((nhp, PEER_N_KEYS, PEER_KEY_HALF), lambda i: (0, 0, 0))],
        out_specs=[blk, blk, blk, blk],
        out_shape=[shp, shp, shp16, shp16],
        scratch_shapes=[pltpu.VMEM((2, PEER_TOPK, PEER_HEADS, t), F32),
                        pltpu.VMEM((PEER_HEADS, PEER_N_KEYS, t), F32)],
        compiler_params=_cparams(("parallel",)),
        name="peer_route",
    )(qry, keys)


def _two_gelu(x):
    return x * (1.0 + lax.erf(x * (2.0 ** -0.5)))


def _row_tile_bf16(row, n):
    packed_rows = 2 * SUBLANES
    one = jnp.broadcast_to(row, (packed_rows, row.shape[1])).astype(BF16)
    return pltpu.repeat(one, n // packed_rows, axis=0)


def _peer_kernel(h2_ref, x1_ref, a0_ref, c0_ref, r1_ref, b1_ref, u_ref, vt_ref, gf_ref, y_ref,
                 xt_scr, acc_scr, g_scr, *, t):
    j = pl.program_id(1)
    nk = PEER_N_KEYS
    ni = a0_ref.shape[1]

    @pl.when(j == 0)
    def _():
        xt_scr[...] = h2_ref[...].T.astype(BF16)
        acc_scr[...] = jnp.zeros_like(acc_scr)

    act = jnp.dot(u_ref[...], xt_scr[...], preferred_element_type=F32)
    for ii in range(ni):
        for lt in range(t // LANES):
            ls = slice(lt * LANES, (lt + 1) * LANES)
            w = jnp.zeros((nk, LANES), BF16)
            for h in range(PEER_HEADS):
                a0 = _row_tile_bf16(a0_ref[h, ii:ii + 1, ls], nk)
                c0 = _row_tile_bf16(c0_ref[h, ii:ii + 1, ls], nk)
                w = w + (a0 * b1_ref[h, :, ls]) * jnp.clip(c0 - r1_ref[h, :, ls], 0.0, 1.0)
            g_scr[ii * nk:(ii + 1) * nk, ls] = _two_gelu(act[ii * nk:(ii + 1) * nk, ls]).astype(BF16) * w
    acc_scr[...] += jnp.dot(vt_ref[...], g_scr[...], preferred_element_type=F32)

    @pl.when(j == pl.num_programs(1) - 1)
    def _():
        y_ref[...] = _rmsnorm(x1_ref[...] + acc_scr[...].T, gf_ref[...])


def _peer(h2, x1, a0, c0, r1, b1, u_bf, vt_bf, gf, t):
    n, d = h2.shape
    ne = u_bf.shape[0]
    ni = 2 * SUBLANES
    e_tile = ni * PEER_N_KEYS
    half0 = pl.BlockSpec((PEER_HEADS, ni, t), lambda i, j: (0, j, i))
    half1 = pl.BlockSpec((PEER_HEADS, PEER_N_KEYS, t), lambda i, j: (0, 0, i))
    return pl.pallas_call(
        functools.partial(_peer_kernel, t=t),
        grid=(n // t, ne // e_tile),
        in_specs=[pl.BlockSpec((t, d), lambda i, j: (i, 0)),
                  pl.BlockSpec((t, d), lambda i, j: (i, 0)),
                  half0, half0, half1, half1,
                  pl.BlockSpec((e_tile, d), lambda i, j: (j, 0)),
                  pl.BlockSpec((d, e_tile), lambda i, j: (0, j)),
                  pl.BlockSpec((1, d), lambda i, j: (0, 0))],
        out_specs=pl.BlockSpec((t, d), lambda i, j: (i, 0)),
        out_shape=jax.ShapeDtypeStruct((n, d), F32),
        scratch_shapes=[pltpu.VMEM((d, t), BF16),
                        pltpu.VMEM((d, t), F32),
                        pltpu.VMEM((e_tile, t), BF16)],
        compiler_params=_cparams(("parallel", "arbitrary")),
        name="peer_experts",
    )(h2, x1, a0, c0, r1, b1, u_bf, vt_bf, gf)


def _token_tile(n, pref):
    t = pref
    while n % t:
        t //= 2
    return t


def _group(x, sb_fn, s0, wts, kv_t):
    b, l, d = x.shape
    n = b * l
    x2d = x.reshape(n, d)
    tm = _token_tile(n, 256)
    qa, ka, va, qkb, vb, rb, la, gates = _in_proj(
        x2d, wts['norm_mix_g'], wts['w_main'], wts['w_kvt'], wts['w_lr'], wts['w2'], wts['b2'], tm, l, kv_t)
    r3 = lambda a: a.reshape(b, l, a.shape[-1])
    if not kv_t:
        ka, va = r3(ka), r3(va)
    oa = sb_fn(r3(qa), ka, va)
    ob, s_fin = _gla(r3(qkb), r3(vb), r3(rb), r3(la), s0, wts['gla_norm_g'])
    x1, h2, qry = _merge(x2d, oa.reshape(n, SB_WIDTH), ob.reshape(n, GLA_VWIDTH), gates,
                         wts['w_sb_up'], wts['w_gla_up'], wts['w_out'], wts['norm_ffn_g'],
                         wts['peer_w_q'], _token_tile(n, 256))
    a0, c0, r1, b1 = _route(qry, wts['peer_keys'], _token_tile(n, 256))
    y = _peer(h2, x1, a0, c0, r1, b1, wts['peer_u'], wts['peer_vt'], wts['norm_out_g'],
              _token_tile(n, 512))
    return y.reshape(b, l, d), ka, va, s_fin


def kernel(x_prompt, x_sample, cache_sb_k, cache_sb_v, page_table, state_gla, norm_mix_g, w_in, sb_bias,
           gla_gate_w2, gla_gate_b, gla_norm_g, w_sb_up, w_gla_up, w_out, norm_ffn_g, peer_w_q, peer_keys,
           peer_u, peer_v, norm_final_g):
    depth = w_in.shape[0]
    assert depth == 1, "the final norm is fused into the last (only) layer"
    bp, lp, d = x_prompt.shape
    bs, lsm, _ = x_sample.shape
    xp, xs = x_prompt, x_sample
    outs = [[] for _ in range(6)]
    for l in range(depth):
        w = w_in[l]
        o_lr = 3 * SB_WIDTH + 2 * GLA_KWIDTH + 2 * GLA_VWIDTH
        wts = dict(
            norm_mix_g=norm_mix_g[l].reshape(1, d),
            w_main=jnp.concatenate([w[:, :o_lr], w[:, o_lr + GLA_GATE_RANK:]], axis=1).astype(BF16),
            w_kvt=w[:, SB_WIDTH:3 * SB_WIDTH].T.astype(BF16),
            w_lr=jnp.pad(w[:, o_lr:o_lr + GLA_GATE_RANK], ((0, 0), (0, LANES - GLA_GATE_RANK))).astype(BF16),
            w2=jnp.pad(gla_gate_w2[l], ((0, LANES - GLA_GATE_RANK), (0, 0))),
            b2=gla_gate_b[l].reshape(1, GLA_KWIDTH),
            gla_norm_g=gla_norm_g[l].reshape(1, GLA_VWIDTH),
            w_sb_up=w_sb_up[l].astype(BF16),
            w_gla_up=w_gla_up[l].astype(BF16),
            w_out=w_out[l].astype(BF16),
            norm_ffn_g=norm_ffn_g[l].reshape(1, d),
            peer_w_q=peer_w_q[l].astype(BF16),
            peer_keys=peer_keys[l].reshape(2 * PEER_HEADS, PEER_N_KEYS, PEER_KEY_HALF),
            peer_u=peer_u[l].astype(BF16),
            peer_vt=peer_v[l].T.astype(BF16),
            norm_out_g=norm_final_g.reshape(1, d),
        )
        bias = sb_bias[l]

        s0p = jnp.zeros((bp, GLA_KWIDTH, GLA_DV), F32)
        xp, kp, vp, sp = _group(xp, lambda q, k, v: _sb_prompt(q, k, v, bias), s0p, wts, True)
        s0s = state_gla[l].reshape(bs, GLA_KWIDTH, GLA_DV)
        xs, ks, vs, ss = _group(
            xs, lambda q, k, v: _sb_sample(q, k, v, bias, cache_sb_k, cache_sb_v, l, page_table), s0s, wts, False)

        outs[0].append(kp.reshape(bp, SB_HEADS, SB_HEAD_DIM, lp).transpose(0, 3, 1, 2))
        outs[1].append(vp.reshape(bp, SB_HEADS, SB_HEAD_DIM, lp).transpose(0, 3, 1, 2))
        outs[2].append(sp.reshape(bp, GLA_HEADS, GLA_DK, GLA_DV))
        outs[3].append(ks.reshape(bs, lsm, SB_HEADS, SB_HEAD_DIM))
        outs[4].append(vs.reshape(bs, lsm, SB_HEADS, SB_HEAD_DIM))
        outs[5].append(ss.reshape(bs, GLA_HEADS, GLA_DK, GLA_DV))
    return (xp, xs) + tuple(o[0][None] for o in outs)
```

```python
import functools

import numpy as np
import jax
import jax.numpy as jnp
from jax import lax
from jax.experimental import pallas as pl
from jax.experimental.pallas import tpu as pltpu

F32 = jnp.float32
BF16 = jnp.bfloat16

NORM_EPS = 1e-6
LOG2E = 1.4426950408889634
SB_HEADS = 8
SB_HEAD_DIM = 64
SB_WIDTH = SB_HEADS * SB_HEAD_DIM
GLA_HEADS = 4
GLA_DK = 64
GLA_DV = 128
GLA_KWIDTH = GLA_HEADS * GLA_DK
GLA_VWIDTH = GLA_HEADS * GLA_DV
GLA_GATE_RANK = 16
GLA_GATE_TEMP = 16.0
PEER_HEADS = 8
PEER_N_KEYS = 128
PEER_KEY_HALF = 128
PEER_TOPK = 16

LANES = 128
SUBLANES = 8
VMEM_LIMIT_BYTES = 56 * 1024 * 1024
GLA_CHUNK = 128
GLA_LEVELS = 7
GLA_SEQS_PER_STEP = 2
SB_TILE = 256
SB_QUERY_TILE = 512
SB_PAGES_PER_STEP = 16
PROJ_TOKEN_TILE = 512

_NT = (((1,), (1,)), ((), ()))
_TN = (((0,), (0,)), ((), ()))


def _cparams(sem):
    return pltpu.CompilerParams(dimension_semantics=sem, vmem_limit_bytes=VMEM_LIMIT_BYTES)


def _rmsnorm(x, g):
    return x * lax.rsqrt(jnp.mean(x * x, axis=-1, keepdims=True) + NORM_EPS) * g


def _neg_softplus(z):
    return -(jnp.maximum(z, 0.0) + jnp.log1p(jnp.exp(-jnp.abs(z))))


def _stick_log2(z2):
    mn = jnp.minimum(z2, 0.0)
    t = jnp.log2(1.0 + jnp.exp2(mn + mn - z2))
    lsig = mn - t
    return lsig, lsig - z2


def _aligned(i, m):
    return i if isinstance(i, int) else pl.multiple_of(i, m)


def _split2(x):
    hi = x.astype(BF16)
    lo = (x - hi.astype(F32)).astype(BF16)
    return hi, lo


def _split3(x):
    hi = x.astype(BF16)
    r = x - hi.astype(F32)
    mid = r.astype(BF16)
    lo = (r - mid.astype(F32)).astype(BF16)
    return hi, mid, lo


def _inproj_kernel(x_ref, g_ref, w_ref, wkvt_ref, wlr_ref, w2_ref, b2_ref,
                   qa_ref, ka_ref, va_ref, qkb_ref, vb_ref, rb_ref, la_ref, gates_ref, *, kv_t):
    h = _rmsnorm(x_ref[...], g_ref[...]).astype(BF16)

    def proj(lo, hi):
        return jnp.dot(h, w_ref[:, lo:hi], preferred_element_type=F32)

    o = 0
    qa_ref[...] = proj(o, o + SB_WIDTH) * (SB_HEAD_DIM ** -0.5 * LOG2E); o += SB_WIDTH
    if kv_t:
        ka_ref[0] = lax.dot_general(wkvt_ref[:SB_WIDTH, :], h, _NT, preferred_element_type=F32)
        va_ref[0] = lax.dot_general(wkvt_ref[SB_WIDTH:, :], h, _NT, preferred_element_type=F32)
    else:
        ka_ref[...] = proj(o, o + SB_WIDTH)
        va_ref[...] = proj(o + SB_WIDTH, o + 2 * SB_WIDTH)
    o += 2 * SB_WIDTH
    qkb_ref[:, :GLA_KWIDTH] = proj(o, o + GLA_KWIDTH) * (GLA_DK ** -0.5); o += GLA_KWIDTH
    qkb_ref[:, GLA_KWIDTH:] = proj(o, o + GLA_KWIDTH); o += GLA_KWIDTH
    vb_ref[...] = proj(o, o + GLA_VWIDTH); o += GLA_VWIDTH
    rb_ref[...] = proj(o, o + GLA_VWIDTH); o += GLA_VWIDTH
    d = gates_ref.shape[1]
    gates_ref[...] = proj(o, o + d)
    gk = jnp.dot(h, wlr_ref[...], preferred_element_type=F32)
    pre = jnp.dot(gk, w2_ref[...], preferred_element_type=F32,
                  precision=lax.Precision.HIGHEST) + b2_ref[...]
    la_ref[...] = _neg_softplus(-pre) * (1.0 / GLA_GATE_TEMP)


def _in_proj(x2d, g, w_main, w_kvt, w_lr, w2, b2, tm, seq_len, kv_t):
    n, d = x2d.shape
    wm = w_main.shape[1]
    const = lambda i: (0, 0)
    row = lambda i: (i, 0)
    widths = [SB_WIDTH, SB_WIDTH, SB_WIDTH, 2 * GLA_KWIDTH, GLA_VWIDTH, GLA_VWIDTH, GLA_KWIDTH, 2 * d]
    out_specs = [pl.BlockSpec((tm, w), row) for w in widths]
    out_shape = [jax.ShapeDtypeStruct((n, w), F32) for w in widths]
    if kv_t:
        per_seq = seq_len // tm
        for i in (1, 2):
            out_specs[i] = pl.BlockSpec((1, SB_WIDTH, tm), lambda i: (i // per_seq, 0, i % per_seq))
            out_shape[i] = jax.ShapeDtypeStruct((n // seq_len, SB_WIDTH, seq_len), F32)
    return pl.pallas_call(
        functools.partial(_inproj_kernel, kv_t=kv_t),
        grid=(n // tm,),
        in_specs=[pl.BlockSpec((tm, d), row),
                  pl.BlockSpec((1, d), const),
                  pl.BlockSpec((d, wm), const, pipeline_mode=pl.Buffered(1)),
                  pl.BlockSpec(w_kvt.shape, const, pipeline_mode=pl.Buffered(1)),
                  pl.BlockSpec((d, LANES), const),
                  pl.BlockSpec((LANES, GLA_KWIDTH), const),
                  pl.BlockSpec((1, GLA_KWIDTH), const)],
        out_specs=out_specs,
        out_shape=out_shape,
        compiler_params=_cparams(("parallel",)),
        name="in_proj",
    )(x2d, g, w_main, w_kvt, w_lr, w2, b2)


def _sb_prompt_kernel(bias_ref, cmat_ref, q_ref, k_ref, v_ref, o_ref, *, tq, t):
    p = pl.program_id(1)
    qi = pl.program_id(2)
    lane = lax.broadcasted_iota(jnp.int32, (1, LANES), 1)
    first = lane < SB_HEAD_DIM
    first_row = lax.broadcasted_iota(jnp.int32, (LANES, 1), 0) < SB_HEAD_DIM
    q = q_ref[0]
    qh = (jnp.where(first, q, 0.0).astype(BF16), jnp.where(first, 0.0, q).astype(BF16))
    bias = (bias_ref[2 * p], bias_ref[2 * p + 1])
    nd = tq // t

    def block(j, carry, d):
        masked = d is not None
        acc, c0, c1 = carry
        off = _aligned(j * t, t)
        kt = k_ref[0, :, pl.ds(off, t)].astype(BF16)
        vt = v_ref[0, :, pl.ds(off, t)]
        vh = (jnp.where(first_row, vt, 0.0).astype(BF16), jnp.where(first_row, 0.0, vt).astype(BF16))
        if masked:
            valid = (lax.broadcasted_iota(jnp.int32, (tq, t), 1) + d * t
                     < lax.broadcasted_iota(jnp.int32, (tq, t), 0))
        cs = [c0, c1]
        for hh in range(2):
            z = jnp.dot(qh[hh], kt, preferred_element_type=F32) + bias[hh]
            lsig, ls = _stick_log2(z)
            if masked:
                ls = jnp.where(valid, ls, 0.0)
            lsb = ls.astype(BF16)
            r = jnp.dot(lsb, cmat_ref[...], preferred_element_type=F32)
            w = jnp.exp2(lsig + r + cs[hh])
            if masked:
                w = jnp.where(valid, w, 0.0)
            acc = acc + lax.dot_general(w.astype(BF16), vh[hh], _NT, preferred_element_type=F32)
            total = r[:, :1] + lsb[:, :1].astype(F32)
            cs[hh] = cs[hh] + jnp.broadcast_to(total, (tq, t))
        return acc, cs[0], cs[1]

    carry = (jnp.zeros((tq, LANES), F32), jnp.zeros((tq, t), F32), jnp.zeros((tq, t), F32))
    for d in reversed(range(nd)):
        carry = block(qi * nd + d, carry, d)
    carry = lax.fori_loop(0, qi * nd, lambda jj, c: block(qi * nd - 1 - jj, c, None), carry)
    o_ref[0] = carry[0]


def _sb_prompt(q, k, v, bias):
    b, l, _ = q.shape
    t = min(SB_TILE, l)
    tq = min(SB_QUERY_TILE, l)
    cmat = jnp.asarray(np.arange(t)[:, None] > np.arange(t)[None, :], BF16)
    return pl.pallas_call(
        functools.partial(_sb_prompt_kernel, tq=tq, t=t),
        grid=(b, SB_WIDTH // LANES, l // tq),
        in_specs=[pl.BlockSpec(memory_space=pltpu.SMEM),
                  pl.BlockSpec((t, t), lambda bi, p, qi: (0, 0)),
                  pl.BlockSpec((1, tq, LANES), lambda bi, p, qi: (bi, qi, p)),
                  pl.BlockSpec((1, LANES, l), lambda bi, p, qi: (bi, p, 0)),
                  pl.BlockSpec((1, LANES, l), lambda bi, p, qi: (bi, p, 0))],
        out_specs=pl.BlockSpec((1, tq, LANES), lambda bi, p, qi: (bi, qi, p)),
        out_shape=jax.ShapeDtypeStruct((b, l, SB_WIDTH), F32),
        compiler_params=_cparams(("parallel", "parallel", "arbitrary")),
        name="sb_prompt",
    )(bias * LOG2E, cmat, q, k, v)


def _sb_sample_kernel(pt_ref, brow_ref, cmat_ref, qbd_ref, kn_ref, vn_ref, *rest, npg, ln):
    del pt_ref
    kp_refs = rest[:npg]
    vp_refs = rest[npg:2 * npg]
    o_ref, acc_ref, carry_ref = rest[2 * npg:]
    g = pl.program_id(1)
    pg = kp_refs[0].shape[3]
    qbd = qbd_ref[0]
    bcol = brow_ref[...]

    nr = SB_HEADS * ln

    def pages(kts, vts, valid):
        zs = [jnp.dot(qbd, kt.astype(BF16), preferred_element_type=F32) + bcol for kt in kts]
        lsigs, lss = zip(*[_stick_log2(z) for z in zs])
        if valid is not None:
            lss = [jnp.where(valid, ls, 0.0) for ls in lss]
        r = jnp.dot(jnp.concatenate(lss, axis=0).astype(BF16), cmat_ref[...], preferred_element_type=F32)
        run = carry_ref[...]
        acc = acc_ref[...]
        for i, (lsig, vt) in enumerate(zip(lsigs, vts)):
            w = jnp.exp2(lsig + r[i * nr:(i + 1) * nr, :pg] + run)
            if valid is not None:
                w = jnp.where(valid, w, 0.0)
            acc = acc + lax.dot_general(w.astype(BF16), vt.astype(BF16), _NT, preferred_element_type=F32)
            run = run + r[i * nr:(i + 1) * nr, pg:]
        acc_ref[...] = acc
        carry_ref[...] = run

    @pl.when(g == 0)
    def _():
        acc_ref[...] = jnp.zeros_like(acc_ref)
        carry_ref[...] = jnp.zeros_like(carry_ref)
        s_lane = lax.broadcasted_iota(jnp.int32, (nr, pg), 1)
        t_row = lax.broadcasted_iota(jnp.int32, (nr, pg), 0) % ln
        pages([kn_ref[0]], [vn_ref[0]], s_lane < t_row)

    pages([r_[0, 0] for r_ in kp_refs], [r_[0, 0] for r_ in vp_refs], None)

    @pl.when(g == pl.num_programs(1) - 1)
    def _():
        lane = lax.broadcasted_iota(jnp.int32, (ln, SB_WIDTH), 1)
        out = jnp.zeros((ln, SB_WIDTH), F32)
        for h in range(SB_HEADS):
            out = out + jnp.where(lane // SB_HEAD_DIM == h, acc_ref[h * ln:(h + 1) * ln, :], 0.0)
        o_ref[0] = out


def _sb_sample(q, k_new, v_new, bias, cache_k, cache_v, layer, page_table):
    bd, ln, _ = q.shape
    n_pages = page_table.shape[1]
    pg = cache_k.shape[2]
    npg = SB_PAGES_PER_STEP
    while n_pages % npg:
        npg //= 2
    depth, n_pool = cache_k.shape[:2]
    nr = SB_HEADS * ln
    assert ln == SUBLANES and ln <= pg
    q4 = q.reshape(bd, ln, SB_HEADS, SB_HEAD_DIM)
    qbd = jnp.einsum('bthd,hg->bhtgd', q4, jnp.eye(SB_HEADS, dtype=F32)).reshape(bd, nr, SB_WIDTH).astype(BF16)
    bcol = jnp.broadcast_to(jnp.repeat(bias * LOG2E, ln)[:, None], (nr, pg))
    later = np.arange(pg)[:, None] > np.arange(pg)[None, :]
    cmat = jnp.asarray(np.concatenate([later, np.ones((pg, pg), bool)], axis=1), BF16)
    ckt = cache_k.transpose(0, 1, 3, 4, 2).reshape(depth, n_pool, SB_WIDTH, pg)
    cvt = cache_v.transpose(0, 1, 3, 4, 2).reshape(depth, n_pool, SB_WIDTH, pg)
    knt = jnp.pad(k_new.transpose(0, 2, 1), ((0, 0), (0, 0), (0, pg - ln)))
    vnt = jnp.pad(v_new.transpose(0, 2, 1), ((0, 0), (0, 0), (0, pg - ln)))

    def page_spec(i):
        return pl.BlockSpec((1, 1, SB_WIDTH, pg),
                            lambda b, g, pt, i=i: (layer, pt[b, n_pages - 1 - (g * npg + i)], 0, 0))

    seq = lambda b, g, pt: (b, 0, 0)
    const = lambda b, g, pt: (0, 0)
    grid_spec = pltpu.PrefetchScalarGridSpec(
        num_scalar_prefetch=1,
        grid=(bd, n_pages // npg),
        in_specs=[pl.BlockSpec((nr, pg), const),
                  pl.BlockSpec((pg, 2 * pg), const),
                  pl.BlockSpec((1, nr, SB_WIDTH), seq),
                  pl.BlockSpec((1, SB_WIDTH, pg), seq),
                  pl.BlockSpec((1, SB_WIDTH, pg), seq)]
                 + [page_spec(i) for i in range(npg)] * 2,
        out_specs=pl.BlockSpec((1, ln, SB_WIDTH), seq),
        scratch_shapes=[pltpu.VMEM((nr, SB_WIDTH), F32), pltpu.VMEM((nr, pg), F32)],
    )
    return pl.pallas_call(
        functools.partial(_sb_sample_kernel, npg=npg, ln=ln),
        grid_spec=grid_spec,
        out_shape=jax.ShapeDtypeStruct((bd, ln, SB_WIDTH), F32),
        compiler_params=_cparams(("parallel", "arbitrary")),
        name="sb_sample",
    )(page_table, bcol, cmat, qbd, knt, vnt, *([ckt] * npg), *([cvt] * npg))


def _gla_constants():
    c = GLA_CHUNK
    t = np.arange(c)[:, None]
    u = np.arange(c)[None, :]
    blocks = [(u <= t), (u > t)]
    level = np.zeros((c, c), np.int32)
    for l in range(1, GLA_LEVELS + 1):
        m = c >> l
        r = (t // (2 * m)) * (2 * m) + m - 1
        upper = (t % (2 * m)) >= m
        blocks.append(upper & (u > r) & (u <= t))
        blocks.append((~upper) & (u > t) & (u <= r))
        tt, ss = np.arange(c)[:, None], np.arange(c)[None, :]
        same = (tt // (2 * m)) == (ss // (2 * m))
        level[same & ((tt % (2 * m)) >= m) & ((ss % (2 * m)) < m)] = l
    level[np.arange(c), np.arange(c)] = GLA_LEVELS + 1
    mall = np.concatenate([b.astype(np.float32) for b in blocks], axis=0)
    return jnp.asarray(mall, BF16), jnp.asarray(level)


def _gla_kernel(mall_ref, level_ref, qk_ref, v_ref, r_ref, la_ref, s0_ref, gn_ref,
                o_ref, sfin_ref, s_scr):
    ci = pl.program_id(1)

    @pl.when(ci == 0)
    def _():
        s_scr[...] = s0_ref[...]

    for bi in range(qk_ref.shape[0]):
        _gla_chunk(bi, mall_ref, level_ref, qk_ref, v_ref, r_ref, la_ref, gn_ref, o_ref, s_scr)

    @pl.when(ci == pl.num_programs(1) - 1)
    def _():
        sfin_ref[...] = s_scr[...]


def _gla_chunk(bi, mall_ref, level_ref, qk_ref, v_ref, r_ref, la_ref, gn_ref, o_ref, s_scr):
    c = GLA_CHUNK
    lb = qk_ref.shape[1]

    def rows(x):
        if lb == c:
            return x
        return jnp.concatenate([x, jnp.zeros((c - lb, x.shape[1]), x.dtype)], axis=0)

    qk = rows(qk_ref[bi])
    q = qk[:, :GLA_KWIDTH]
    k = qk[:, GLA_KWIDTH:]
    v = rows(v_ref[bi])
    la = rows(la_ref[bi])
    hi, mid, lo = _split3(la)
    mall = mall_ref[...]
    cum = (jnp.dot(mall, hi, preferred_element_type=F32)
           + jnp.dot(mall, mid, preferred_element_type=F32)
           + jnp.dot(mall, lo, preferred_element_type=F32))
    pw = jnp.exp(cum)

    def blk(i):
        return pw[i * c:(i + 1) * c]

    level = level_ref[...]
    lane = lax.broadcasted_iota(jnp.int32, (1, LANES), 1)
    rowi = lax.broadcasted_iota(jnp.int32, (LANES, 1), 0)
    qb = q * blk(0)
    kf = k * blk(1)
    for pp in range(GLA_KWIDTH // LANES):
        sl = slice(pp * LANES, (pp + 1) * LANES)
        s_pair = s_scr[bi, sl, :]
        kft = kf[:, sl].T.astype(BF16)
        decay = jnp.exp(jnp.sum(la[:, sl].T, axis=1, keepdims=True))
        s_new = decay * s_pair
        hms = (lane < GLA_DK, lane >= GLA_DK)
        qm2 = jnp.concatenate([jnp.where(hm, q[:, sl], 0.0) for hm in hms], axis=0)
        level2 = jnp.concatenate([level, level], axis=0)
        att2 = jnp.where(level2 == GLA_LEVELS + 1,
                         lax.dot_general(qm2.astype(BF16), k[:, sl].astype(BF16), _NT,
                                         preferred_element_type=F32), 0.0)
        for l in range(1, GLA_LEVELS + 1):
            pd = blk(2 * l)[:, sl]
            qd = (qm2 * jnp.concatenate([pd, pd], axis=0)).astype(BF16)
            ke = (k[:, sl] * blk(2 * l + 1)[:, sl]).astype(BF16)
            att2 = att2 + jnp.where(level2 == l,
                                    lax.dot_general(qd, ke, _NT, preferred_element_type=F32), 0.0)
        for hh in range(2):
            h = 2 * pp + hh
            hm = hms[hh]
            vh = v[:, h * GLA_DV:(h + 1) * GLA_DV].astype(BF16)
            att = att2[hh * c:(hh + 1) * c]
            o = jnp.dot(att.astype(BF16), vh, preferred_element_type=F32)
            o = o + jnp.dot(jnp.where(hm, qb[:, sl], 0.0).astype(BF16), s_pair.astype(BF16),
                            preferred_element_type=F32)
            o = _rmsnorm(o, gn_ref[:, h * GLA_DV:(h + 1) * GLA_DV])
            rg = rows(r_ref[bi][:, h * GLA_DV:(h + 1) * GLA_DV])
            o = o * (rg * jax.nn.sigmoid(rg))
            o_ref[bi, :, h * GLA_DV:(h + 1) * GLA_DV] = o[:lb]
            rm = (rowi < GLA_DK) if hh == 0 else (rowi >= GLA_DK)
            s_new = s_new + jnp.where(rm, jnp.dot(kft, vh, preferred_element_type=F32), 0.0)
        s_scr[bi, sl, :] = s_new


def _gla(qk, v, r, la, s0, gn):
    b, l, _ = qk.shape
    c = GLA_CHUNK
    lb = c if l % c == 0 else l
    assert lb == c or (l < c and l % SUBLANES == 0)
    mall, level = _gla_constants()
    nb = GLA_SEQS_PER_STEP if b % GLA_SEQS_PER_STEP == 0 else 1
    const = lambda bi, ci: (0, 0)
    tok = lambda bi, ci: (bi, ci, 0)
    seq = lambda bi, ci: (bi, 0, 0)
    return pl.pallas_call(
        _gla_kernel,
        grid=(b // nb, l // lb),
        in_specs=[pl.BlockSpec(mall.shape, const),
                  pl.BlockSpec(level.shape, const),
                  pl.BlockSpec((nb, lb, 2 * GLA_KWIDTH), tok),
                  pl.BlockSpec((nb, lb, GLA_VWIDTH), tok),
                  pl.BlockSpec((nb, lb, GLA_VWIDTH), tok),
                  pl.BlockSpec((nb, lb, GLA_KWIDTH), tok),
                  pl.BlockSpec((nb, GLA_KWIDTH, GLA_DV), seq),
                  pl.BlockSpec((1, GLA_VWIDTH), const)],
        out_specs=[pl.BlockSpec((nb, lb, GLA_VWIDTH), tok),
                   pl.BlockSpec((nb, GLA_KWIDTH, GLA_DV), seq)],
        out_shape=[jax.ShapeDtypeStruct((b, l, GLA_VWIDTH), F32),
                   jax.ShapeDtypeStruct((b, GLA_KWIDTH, GLA_DV), F32)],
        scratch_shapes=[pltpu.VMEM((nb, GLA_KWIDTH, GLA_DV), F32)],
        compiler_params=_cparams(("parallel", "arbitrary")),
        name="gla",
    )(mall, level, qk, v, r, la, s0, gn)


def _merge_kernel(x_ref, oa_ref, ob_ref, gates_ref, wsb_ref, wgl_ref, wout_ref, g2_ref, wq_ref,
                  x1_ref, h2_ref, qry_ref):
    d = x_ref.shape[1]
    ua = jnp.dot(oa_ref[...].astype(BF16), wsb_ref[...], preferred_element_type=F32)
    ub = jnp.dot(ob_ref[...].astype(BF16), wgl_ref[...], preferred_element_type=F32)
    merged = jax.nn.sigmoid(gates_ref[:, :d]) * ua + jax.nn.sigmoid(gates_ref[:, d:]) * ub
    x1 = x_ref[...] + jnp.dot(merged.astype(BF16), wout_ref[...], preferred_element_type=F32)
    x1_ref[...] = x1
    h2 = _rmsnorm(x1, g2_ref[...])
    h2_ref[...] = h2
    qry_ref[...] = jnp.dot(h2.astype(BF16), wq_ref[...], preferred_element_type=F32)


def _merge(x2d, oa, ob, gates, wsb, wgl, wout, g2, wq, tm):
    n, d = x2d.shape
    nq = wq.shape[1]
    const = lambda i: (0, 0)
    row = lambda i: (i, 0)
    return pl.pallas_call(
        _merge_kernel,
        grid=(n // tm,),
        in_specs=[pl.BlockSpec((tm, d), row),
                  pl.BlockSpec((tm, SB_WIDTH), row),
                  pl.BlockSpec((tm, GLA_VWIDTH), row),
                  pl.BlockSpec((tm, 2 * d), row),
                  pl.BlockSpec(wsb.shape, const, pipeline_mode=pl.Buffered(1)),
                  pl.BlockSpec(wgl.shape, const, pipeline_mode=pl.Buffered(1)),
                  pl.BlockSpec(wout.shape, const, pipeline_mode=pl.Buffered(1)),
                  pl.BlockSpec((1, d), const),
                  pl.BlockSpec(wq.shape, const, pipeline_mode=pl.Buffered(1))],
        out_specs=[pl.BlockSpec((tm, d), row), pl.BlockSpec((tm, d), row), pl.BlockSpec((tm, nq), row)],
        out_shape=[jax.ShapeDtypeStruct((n, d), F32), jax.ShapeDtypeStruct((n, d), F32),
                   jax.ShapeDtypeStruct((n, nq), F32)],
        compiler_params=_cparams(("parallel",)),
        name="merge",
    )(x2d, oa, ob, gates, wsb, wgl, wout, g2, wq)


RANK_CODE_BASE = 2.0 ** 126
RANK_CODE_STEP = 2.0 ** 106


def _peer_candidates():
    return [(a, b) for a in range(PEER_TOPK) for b in range(PEER_TOPK) if (a + 1) * (b + 1) <= PEER_TOPK]


def _route_kernel(qry_ref, keys_ref, a0_ref, c0_ref, r1_ref, b1_ref, top_scr, rank_scr):
    neg = -jnp.inf
    for h in range(PEER_HEADS):
        for p in range(2):
            hp = 2 * h + p
            qh = qry_ref[:, hp * PEER_KEY_HALF:(hp + 1) * PEER_KEY_HALF]
            s = lax.dot_general(keys_ref[hp], qh, _NT, preferred_element_type=F32,
                                precision=lax.Precision.HIGHEST)
            cur = s
            for r in range(PEER_TOPK):
                m = jnp.max(cur, axis=0, keepdims=True)
                top_scr[p, r, h:h + 1, :] = m
                cur = jnp.where(cur >= m, -(RANK_CODE_BASE + r * RANK_CODE_STEP), cur)
            rank = jnp.where(cur <= -RANK_CODE_BASE, (-cur - RANK_CODE_BASE) * (1.0 / RANK_CODE_STEP),
                             float(PEER_TOPK))
            e = jnp.exp(s - top_scr[p, 0, h:h + 1, :])
            if p == 0:
                rank_scr[h] = rank
                a0_ref[h] = e
            else:
                r1_ref[h] = rank.astype(BF16)
                b1_ref[h] = e.astype(BF16)
    pairs = _peer_candidates()
    cands = [top_scr[0, a] + top_scr[1, b] for (a, b) in pairs]
    cmax = cands[0]
    zsum = jnp.zeros_like(cmax)
    tau = cmax
    for r in range(PEER_TOPK):
        tau = functools.reduce(jnp.maximum, cands)
        zsum = zsum + jnp.exp(tau - cmax)
        cands = [jnp.where(cd >= tau, neg, cd) for cd in cands]
    half_inv_z = 0.5 / zsum
    cnt = [jnp.zeros_like(cmax) for _ in range(PEER_TOPK)]
    for (a, b) in pairs:
        cnt[a] = cnt[a] + jnp.where(top_scr[0, a] + top_scr[1, b] >= tau, 1.0, 0.0)
    for h in range(PEER_HEADS):
        rank = rank_scr[h]
        c0 = jnp.zeros_like(rank)
        for a in range(PEER_TOPK):
            c0 = jnp.where(rank == float(a), cnt[a][h:h + 1, :], c0)
        c0_ref[h] = c0
        a0_ref[h] = a0_ref[h] * half_inv_z[h:h + 1, :]


def _route(qry, keys, t):
    n = qry.shape[0]
    nhp = 2 * PEER_HEADS
    blk = pl.BlockSpec((PEER_HEADS, PEER_N_KEYS, t), lambda i: (0, 0, i))
    shp = jax.ShapeDtypeStruct((PEER_HEADS, PEER_N_KEYS, n), F32)
    shp16 = jax.ShapeDtypeStruct((PEER_HEADS, PEER_N_KEYS, n), BF16)
    return pl.pallas_call(
        _route_kernel,
        grid=(n // t,),
        in_specs=[pl.BlockSpec((t, nhp * PEER_KEY_HALF), lambda i: (i, 0)),
                  pl.BlockSpec((nhp, PEER_N_KEYS, PEER_KEY_HALF), lambda i: (0, 0, 0))],
        out_specs=[blk, blk, blk, blk],
        out_shape=[shp, shp, shp16, shp16],
        scratch_shapes=[pltpu.VMEM((2, PEER_TOPK, PEER_HEADS, t), F32),
                        pltpu.VMEM((PEER_HEADS, PEER_N_KEYS, t), F32)],
        compiler_params=_cparams(("parallel",)),
        name="peer_route",
    )(qry, keys)


def _two_gelu(x):
    return x * (1.0 + lax.erf(x * (2.0 ** -0.5)))


def _row_tile_bf16(row, n):
    packed_rows = 2 * SUBLANES
    one = jnp.broadcast_to(row, (packed_rows, row.shape[1])).astype(BF16)
    return pltpu.repeat(one, n // packed_rows, axis=0)


def _peer_kernel(h2_ref, x1_ref, a0_ref, c0_ref, r1_ref, b1_ref, u_ref, vt_ref, gf_ref, y_ref,
                 xt_scr, acc_scr, g_scr, *, t):
    j = pl.program_id(1)
    nk = PEER_N_KEYS
    ni = a0_ref.shape[1]

    @pl.when(j == 0)
    def _():
        xt_scr[...] = h2_ref[...].T.astype(BF16)
        acc_scr[...] = jnp.zeros_like(acc_scr)

    act = jnp.dot(u_ref[...], xt_scr[...], preferred_element_type=F32)
    for ii in range(ni):
        for lt in range(t // LANES):
            ls = slice(lt * LANES, (lt + 1) * LANES)
            w = jnp.zeros((nk, LANES), BF16)
            for h in range(PEER_HEADS):
                a0 = _row_tile_bf16(a0_ref[h, ii:ii + 1, ls], nk)
                c0 = _row_tile_bf16(c0_ref[h, ii:ii + 1, ls], nk)
                w = w + (a0 * b1_ref[h, :, ls]) * jnp.clip(c0 - r1_ref[h, :, ls], 0.0, 1.0)
            g_scr[ii * nk:(ii + 1) * nk, ls] = _two_gelu(act[ii * nk:(ii + 1) * nk, ls]).astype(BF16) * w
    acc_scr[...] += jnp.dot(vt_ref[...], g_scr[...], preferred_element_type=F32)

    @pl.when(j == pl.num_programs(1) - 1)
    def _():
        y_ref[...] = _rmsnorm(x1_ref[...] + acc_scr[...].T, gf_ref[...])


def _peer(h2, x1, a0, c0, r1, b1, u_bf, vt_bf, gf, t):
    n, d = h2.shape
    ne = u_bf.shape[0]
    ni = 2 * SUBLANES
    e_tile = ni * PEER_N_KEYS
    half0 = pl.BlockSpec((PEER_HEADS, ni, t), lambda i, j: (0, j, i))
    half1 = pl.BlockSpec((PEER_HEADS, PEER_N_KEYS, t), lambda i, j: (0, 0, i))
    return pl.pallas_call(
        functools.partial(_peer_kernel, t=t),
        grid=(n // t, ne // e_tile),
        in_specs=[pl.BlockSpec((t, d), lambda i, j: (i, 0)),
                  pl.BlockSpec((t, d), lambda i, j: (i, 0)),
                  half0, half0, half1, half1,
                  pl.BlockSpec((e_tile, d), lambda i, j: (j, 0)),
                  pl.BlockSpec((d, e_tile), lambda i, j: (0, j)),
                  pl.BlockSpec((1, d), lambda i, j: (0, 0))],
        out_specs=pl.BlockSpec((t, d), lambda i, j: (i, 0)),
        out_shape=jax.ShapeDtypeStruct((n, d), F32),
        scratch_shapes=[pltpu.VMEM((d, t), BF16),
                        pltpu.VMEM((d, t), F32),
                        pltpu.VMEM((e_tile, t), BF16)],
        compiler_params=_cparams(("parallel", "arbitrary")),
        name="peer_experts",
    )(h2, x1, a0, c0, r1, b1, u_bf, vt_bf, gf)


def _token_tile(n, pref):
    t = pref
    while n % t:
        t //= 2
    return t


def _group(x, sb_fn, s0, wts, kv_t):
    b, l, d = x.shape
    n = b * l
    x2d = x.reshape(n, d)
    tm = _token_tile(l if kv_t else n, PROJ_TOKEN_TILE)
    qa, ka, va, qkb, vb, rb, la, gates = _in_proj(
        x2d, wts['norm_mix_g'], wts['w_main'], wts['w_kvt'], wts['w_lr'], wts['w2'], wts['b2'], tm, l, kv_t)
    r3 = lambda a: a.reshape(b, l, a.shape[-1])
    if not kv_t:
        ka, va = r3(ka), r3(va)
    oa = sb_fn(r3(qa), ka, va)
    ob, s_fin = _gla(r3(qkb), r3(vb), r3(rb), r3(la), s0, wts['gla_norm_g'])
    x1, h2, qry = _merge(x2d, oa.reshape(n, SB_WIDTH), ob.reshape(n, GLA_VWIDTH), gates,
                         wts['w_sb_up'], wts['w_gla_up'], wts['w_out'], wts['norm_ffn_g'],
                         wts['peer_w_q'], _token_tile(n, PROJ_TOKEN_TILE))
    a0, c0, r1, b1 = _route(qry, wts['peer_keys'], _token_tile(n, 256))
    y = _peer(h2, x1, a0, c0, r1, b1, wts['peer_u'], wts['peer_vt'], wts['norm_out_g'],
              _token_tile(n, 512))
    return y.reshape(b, l, d), ka, va, s_fin


def kernel(x_prompt, x_sample, cache_sb_k, cache_sb_v, page_table, state_gla, norm_mix_g, w_in, sb_bias,
           gla_gate_w2, gla_gate_b, gla_norm_g, w_sb_up, w_gla_up, w_out, norm_ffn_g, peer_w_q, peer_keys,
           peer_u, peer_v, norm_final_g):
    depth = w_in.shape[0]
    assert depth == 1, "the final norm is fused into the last (only) layer"
    bp, lp, d = x_prompt.shape
    bs, lsm, _ = x_sample.shape
    xp, xs = x_prompt, x_sample
    outs = [[] for _ in range(6)]
    for l in range(depth):
        w = w_in[l]
        o_lr = 3 * SB_WIDTH + 2 * GLA_KWIDTH + 2 * GLA_VWIDTH
        wts = dict(
            norm_mix_g=norm_mix_g[l].reshape(1, d),
            w_main=jnp.concatenate([w[:, :o_lr], w[:, o_lr + GLA_GATE_RANK:]], axis=1).astype(BF16),
            w_kvt=w[:, SB_WIDTH:3 * SB_WIDTH].T.astype(BF16),
            w_lr=jnp.pad(w[:, o_lr:o_lr + GLA_GATE_RANK], ((0, 0), (0, LANES - GLA_GATE_RANK))).astype(BF16),
            w2=jnp.pad(gla_gate_w2[l], ((0, LANES - GLA_GATE_RANK), (0, 0))),
            b2=gla_gate_b[l].reshape(1, GLA_KWIDTH),
            gla_norm_g=gla_norm_g[l].reshape(1, GLA_VWIDTH),
            w_sb_up=w_sb_up[l].astype(BF16),
            w_gla_up=w_gla_up[l].astype(BF16),
            w_out=w_out[l].astype(BF16),
            norm_ffn_g=norm_ffn_g[l].reshape(1, d),
            peer_w_q=peer_w_q[l].astype(BF16),
            peer_keys=peer_keys[l].reshape(2 * PEER_HEADS, PEER_N_KEYS, PEER_KEY_HALF),
            peer_u=peer_u[l].astype(BF16),
            peer_vt=peer_v[l].T.astype(BF16),
            norm_out_g=norm_final_g.reshape(1, d),
        )
        bias = sb_bias[l]

        s0p = jnp.zeros((bp, GLA_KWIDTH, GLA_DV), F32)
        xp, kp, vp, sp = _group(xp, lambda q, k, v: _sb_prompt(q, k, v, bias), s0p, wts, True)
        s0s = state_gla[l].reshape(bs, GLA_KWIDTH, GLA_DV)
        xs, ks, vs, ss = _group(
            xs, lambda q, k, v: _sb_sample(q, k, v, bias, cache_sb_k, cache_sb_v, l, page_table), s0s, wts, False)

        outs[0].append(kp.reshape(bp, SB_HEADS, SB_HEAD_DIM, lp).transpose(0, 3, 1, 2))
        outs[1].append(vp.reshape(bp, SB_HEADS, SB_HEAD_DIM, lp).transpose(0, 3, 1, 2))
        outs[2].append(sp.reshape(bp, GLA_HEADS, GLA_DK, GLA_DV))
        outs[3].append(ks.reshape(bs, lsm, SB_HEADS, SB_HEAD_DIM))
        outs[4].append(vs.reshape(bs, lsm, SB_HEADS, SB_HEAD_DIM))
        outs[5].append(ss.reshape(bs, GLA_HEADS, GLA_DK, GLA_DV))
    return (xp, xs) + tuple(o[0][None] for o in outs)
```

```python
import functools

import numpy as np
import jax
import jax.numpy as jnp
from jax import lax
from jax.experimental import pallas as pl
from jax.experimental.pallas import tpu as pltpu

F32 = jnp.float32
BF16 = jnp.bfloat16

NORM_EPS = 1e-6
LOG2E = 1.4426950408889634
SB_HEADS = 8
SB_HEAD_DIM = 64
SB_WIDTH = SB_HEADS * SB_HEAD_DIM
GLA_HEADS = 4
GLA_DK = 64
GLA_DV = 128
GLA_KWIDTH = GLA_HEADS * GLA_DK
GLA_VWIDTH = GLA_HEADS * GLA_DV
GLA_GATE_RANK = 16
GLA_GATE_TEMP = 16.0
PEER_HEADS = 8
PEER_N_KEYS = 128
PEER_KEY_HALF = 128
PEER_TOPK = 16

LANES = 128
SUBLANES = 8
VMEM_LIMIT_BYTES = 56 * 1024 * 1024
GLA_CHUNK = 128
GLA_LEVELS = 7
GLA_SEQS_PER_STEP = 2
SB_TILE = 256
SB_QUERY_TILE = 512
SB_PAGES_PER_STEP = 16
PROJ_TOKEN_TILE = 512

_NT = (((1,), (1,)), ((), ()))
_TN = (((0,), (0,)), ((), ()))


def _cparams(sem):
    return pltpu.CompilerParams(dimension_semantics=sem, vmem_limit_bytes=VMEM_LIMIT_BYTES)


def _rmsnorm(x, g):
    return x * lax.rsqrt(jnp.mean(x * x, axis=-1, keepdims=True) + NORM_EPS) * g


def _neg_softplus(z):
    return -(jnp.maximum(z, 0.0) + jnp.log1p(jnp.exp(-jnp.abs(z))))


def _stick_log2(z2):
    mn = jnp.minimum(z2, 0.0)
    t = jnp.log2(1.0 + jnp.exp2(mn + mn - z2))
    lsig = mn - t
    return lsig, lsig - z2


def _aligned(i, m):
    return i if isinstance(i, int) else pl.multiple_of(i, m)


def _split2(x):
    hi = x.astype(BF16)
    lo = (x - hi.astype(F32)).astype(BF16)
    return hi, lo


def _split3(x):
    hi = x.astype(BF16)
    r = x - hi.astype(F32)
    mid = r.astype(BF16)
    lo = (r - mid.astype(F32)).astype(BF16)
    return hi, mid, lo


def _inproj_kernel(x_ref, g_ref, w_ref, wkvt_ref, wlr_ref, w2_ref, b2_ref,
                   qa_ref, ka_ref, va_ref, qkb_ref, vb_ref, rb_ref, la_ref, gates_ref, *, kv_t):
    h = _rmsnorm(x_ref[...], g_ref[...]).astype(BF16)

    def proj(lo, hi):
        return jnp.dot(h, w_ref[:, lo:hi], preferred_element_type=F32)

    o = 0
    qa_ref[...] = proj(o, o + SB_WIDTH) * (SB_HEAD_DIM ** -0.5 * LOG2E); o += SB_WIDTH
    if kv_t:
        ka_ref[0] = lax.dot_general(wkvt_ref[:SB_WIDTH, :], h, _NT, preferred_element_type=F32)
        va_ref[0] = lax.dot_general(wkvt_ref[SB_WIDTH:, :], h, _NT, preferred_element_type=F32)
    else:
        ka_ref[...] = proj(o, o + SB_WIDTH)
        va_ref[...] = proj(o + SB_WIDTH, o + 2 * SB_WIDTH)
    o += 2 * SB_WIDTH
    qkb_ref[:, :GLA_KWIDTH] = proj(o, o + GLA_KWIDTH) * (GLA_DK ** -0.5); o += GLA_KWIDTH
    qkb_ref[:, GLA_KWIDTH:] = proj(o, o + GLA_KWIDTH); o += GLA_KWIDTH
    vb_ref[...] = proj(o, o + GLA_VWIDTH); o += GLA_VWIDTH
    rb_ref[...] = proj(o, o + GLA_VWIDTH); o += GLA_VWIDTH
    d = gates_ref.shape[1]
    gates_ref[...] = proj(o, o + d)
    gk = jnp.dot(h, wlr_ref[...], preferred_element_type=F32)
    pre = jnp.dot(gk, w2_ref[...], preferred_element_type=F32,
                  precision=lax.Precision.HIGHEST) + b2_ref[...]
    la_ref[...] = _neg_softplus(-pre) * (1.0 / GLA_GATE_TEMP)


def _in_proj(x2d, g, w_main, w_kvt, w_lr, w2, b2, tm, seq_len, kv_t):
    n, d = x2d.shape
    wm = w_main.shape[1]
    const = lambda i: (0, 0)
    row = lambda i: (i, 0)
    widths = [SB_WIDTH, SB_WIDTH, SB_WIDTH, 2 * GLA_KWIDTH, GLA_VWIDTH, GLA_VWIDTH, GLA_KWIDTH, 2 * d]
    out_specs = [pl.BlockSpec((tm, w), row) for w in widths]
    out_shape = [jax.ShapeDtypeStruct((n, w), F32) for w in widths]
    if kv_t:
        per_seq = seq_len // tm
        for i in (1, 2):
            out_specs[i] = pl.BlockSpec((1, SB_WIDTH, tm), lambda i: (i // per_seq, 0, i % per_seq))
            out_shape[i] = jax.ShapeDtypeStruct((n // seq_len, SB_WIDTH, seq_len), F32)
    return pl.pallas_call(
        functools.partial(_inproj_kernel, kv_t=kv_t),
        grid=(n // tm,),
        in_specs=[pl.BlockSpec((tm, d), row),
                  pl.BlockSpec((1, d), const),
                  pl.BlockSpec((d, wm), const, pipeline_mode=pl.Buffered(1)),
                  pl.BlockSpec(w_kvt.shape, const, pipeline_mode=pl.Buffered(1)),
                  pl.BlockSpec((d, LANES), const),
                  pl.BlockSpec((LANES, GLA_KWIDTH), const),
                  pl.BlockSpec((1, GLA_KWIDTH), const)],
        out_specs=out_specs,
        out_shape=out_shape,
        compiler_params=_cparams(("parallel",)),
        name="in_proj",
    )(x2d, g, w_main, w_kvt, w_lr, w2, b2)


def _sb_prompt_kernel(bias_ref, cmat_ref, q_ref, k_ref, v_ref, o_ref, *, tq, t):
    p = pl.program_id(1)
    qi = pl.program_id(2)
    lane = lax.broadcasted_iota(jnp.int32, (1, LANES), 1)
    first = lane < SB_HEAD_DIM
    first_row = lax.broadcasted_iota(jnp.int32, (LANES, 1), 0) < SB_HEAD_DIM
    q = q_ref[0]
    qh = (jnp.where(first, q, 0.0).astype(BF16), jnp.where(first, 0.0, q).astype(BF16))
    bias = (bias_ref[2 * p], bias_ref[2 * p + 1])
    nd = tq // t

    def block(j, carry, d):
        masked = d is not None
        acc, c0, c1 = carry
        off = _aligned(j * t, t)
        kt = k_ref[0, :, pl.ds(off, t)].astype(BF16)
        vt = v_ref[0, :, pl.ds(off, t)]
        vh = (jnp.where(first_row, vt, 0.0).astype(BF16), jnp.where(first_row, 0.0, vt).astype(BF16))
        if masked:
            valid = (lax.broadcasted_iota(jnp.int32, (tq, t), 1) + d * t
                     < lax.broadcasted_iota(jnp.int32, (tq, t), 0))
        cs = [c0, c1]
        for hh in range(2):
            z = jnp.dot(qh[hh], kt, preferred_element_type=F32) + bias[hh]
            lsig, ls = _stick_log2(z)
            if masked:
                ls = jnp.where(valid, ls, 0.0)
            lsb = ls.astype(BF16)
            r = jnp.dot(lsb, cmat_ref[...], preferred_element_type=F32)
            w = jnp.exp2(lsig + r + cs[hh])
            if masked:
                w = jnp.where(valid, w, 0.0)
            acc = acc + lax.dot_general(w.astype(BF16), vh[hh], _NT, preferred_element_type=F32)
            total = r[:, :1] + lsb[:, :1].astype(F32)
            cs[hh] = cs[hh] + jnp.broadcast_to(total, (tq, t))
        return acc, cs[0], cs[1]

    carry = (jnp.zeros((tq, LANES), F32), jnp.zeros((tq, t), F32), jnp.zeros((tq, t), F32))
    for d in reversed(range(nd)):
        carry = block(qi * nd + d, carry, d)
    carry = lax.fori_loop(0, qi * nd, lambda jj, c: block(qi * nd - 1 - jj, c, None), carry)
    o_ref[0] = carry[0]


def _sb_prompt(q, k, v, bias):
    b, l, _ = q.shape
    t = min(SB_TILE, l)
    tq = min(SB_QUERY_TILE, l)
    cmat = jnp.asarray(np.arange(t)[:, None] > np.arange(t)[None, :], BF16)
    return pl.pallas_call(
        functools.partial(_sb_prompt_kernel, tq=tq, t=t),
        grid=(b, SB_WIDTH // LANES, l // tq),
        in_specs=[pl.BlockSpec(memory_space=pltpu.SMEM),
                  pl.BlockSpec((t, t), lambda bi, p, qi: (0, 0)),
                  pl.BlockSpec((1, tq, LANES), lambda bi, p, qi: (bi, qi, p)),
                  pl.BlockSpec((1, LANES, l), lambda bi, p, qi: (bi, p, 0)),
                  pl.BlockSpec((1, LANES, l), lambda bi, p, qi: (bi, p, 0))],
        out_specs=pl.BlockSpec((1, tq, LANES), lambda bi, p, qi: (bi, qi, p)),
        out_shape=jax.ShapeDtypeStruct((b, l, SB_WIDTH), F32),
        compiler_params=_cparams(("parallel", "parallel", "arbitrary")),
        name="sb_prompt",
    )(bias * LOG2E, cmat, q, k, v)


def _sb_sample_kernel(pt_ref, brow_ref, cmat_ref, qbd_ref, kn_ref, vn_ref, *rest, npg, ln):
    del pt_ref
    kp_refs = rest[:npg]
    vp_refs = rest[npg:2 * npg]
    o_ref, acc_ref, carry_ref = rest[2 * npg:]
    g = pl.program_id(1)
    pg = kp_refs[0].shape[3]
    qbd = qbd_ref[0]
    bcol = brow_ref[...]

    nr = SB_HEADS * ln

    def pages(kts, vts, valid):
        zs = [jnp.dot(qbd, kt.astype(BF16), preferred_element_type=F32) + bcol for kt in kts]
        lsigs, lss = zip(*[_stick_log2(z) for z in zs])
        if valid is not None:
            lss = [jnp.where(valid, ls, 0.0) for ls in lss]
        r = jnp.dot(jnp.concatenate(lss, axis=0).astype(BF16), cmat_ref[...], preferred_element_type=F32)
        run = carry_ref[...]
        acc = acc_ref[...]
        for i, (lsig, vt) in enumerate(zip(lsigs, vts)):
            w = jnp.exp2(lsig + r[i * nr:(i + 1) * nr, :pg] + run)
            if valid is not None:
                w = jnp.where(valid, w, 0.0)
            acc = acc + lax.dot_general(w.astype(BF16), vt.astype(BF16), _NT, preferred_element_type=F32)
            run = run + r[i * nr:(i + 1) * nr, pg:]
        acc_ref[...] = acc
        carry_ref[...] = run

    @pl.when(g == 0)
    def _():
        acc_ref[...] = jnp.zeros_like(acc_ref)
        carry_ref[...] = jnp.zeros_like(carry_ref)
        s_lane = lax.broadcasted_iota(jnp.int32, (nr, pg), 1)
        t_row = lax.broadcasted_iota(jnp.int32, (nr, pg), 0) % ln
        pages([kn_ref[0]], [vn_ref[0]], s_lane < t_row)

    pages([r_[0, 0] for r_ in kp_refs], [r_[0, 0] for r_ in vp_refs], None)

    @pl.when(g == pl.num_programs(1) - 1)
    def _():
        lane = lax.broadcasted_iota(jnp.int32, (ln, SB_WIDTH), 1)
        out = jnp.zeros((ln, SB_WIDTH), F32)
        for h in range(SB_HEADS):
            out = out + jnp.where(lane // SB_HEAD_DIM == h, acc_ref[h * ln:(h + 1) * ln, :], 0.0)
        o_ref[0] = out


def _sb_sample(q, k_new, v_new, bias, cache_k, cache_v, layer, page_table):
    bd, ln, _ = q.shape
    n_pages = page_table.shape[1]
    pg = cache_k.shape[2]
    npg = SB_PAGES_PER_STEP
    while n_pages % npg:
        npg //= 2
    depth, n_pool = cache_k.shape[:2]
    nr = SB_HEADS * ln
    assert ln == SUBLANES and ln <= pg
    q4 = q.reshape(bd, ln, SB_HEADS, SB_HEAD_DIM)
    qbd = jnp.einsum('bthd,hg->bhtgd', q4, jnp.eye(SB_HEADS, dtype=F32)).reshape(bd, nr, SB_WIDTH).astype(BF16)
    bcol = jnp.broadcast_to(jnp.repeat(bias * LOG2E, ln)[:, None], (nr, pg))
    later = np.arange(pg)[:, None] > np.arange(pg)[None, :]
    cmat = jnp.asarray(np.concatenate([later, np.ones((pg, pg), bool)], axis=1), BF16)
    ckt = cache_k.transpose(0, 1, 3, 4, 2).reshape(depth, n_pool, SB_WIDTH, pg)
    cvt = cache_v.transpose(0, 1, 3, 4, 2).reshape(depth, n_pool, SB_WIDTH, pg)
    knt = jnp.pad(k_new.transpose(0, 2, 1), ((0, 0), (0, 0), (0, pg - ln)))
    vnt = jnp.pad(v_new.transpose(0, 2, 1), ((0, 0), (0, 0), (0, pg - ln)))

    def page_spec(i):
        return pl.BlockSpec((1, 1, SB_WIDTH, pg),
                            lambda b, g, pt, i=i: (layer, pt[b, n_pages - 1 - (g * npg + i)], 0, 0))

    seq = lambda b, g, pt: (b, 0, 0)
    const = lambda b, g, pt: (0, 0)
    grid_spec = pltpu.PrefetchScalarGridSpec(
        num_scalar_prefetch=1,
        grid=(bd, n_pages // npg),
        in_specs=[pl.BlockSpec((nr, pg), const),
                  pl.BlockSpec((pg, 2 * pg), const),
                  pl.BlockSpec((1, nr, SB_WIDTH), seq),
                  pl.BlockSpec((1, SB_WIDTH, pg), seq),
                  pl.BlockSpec((1, SB_WIDTH, pg), seq)]
                 + [page_spec(i) for i in range(npg)] * 2,
        out_specs=pl.BlockSpec((1, ln, SB_WIDTH), seq),
        scratch_shapes=[pltpu.VMEM((nr, SB_WIDTH), F32), pltpu.VMEM((nr, pg), F32)],
    )
    return pl.pallas_call(
        functools.partial(_sb_sample_kernel, npg=npg, ln=ln),
        grid_spec=grid_spec,
        out_shape=jax.ShapeDtypeStruct((bd, ln, SB_WIDTH), F32),
        compiler_params=_cparams(("parallel", "arbitrary")),
        name="sb_sample",
    )(page_table, bcol, cmat, qbd, knt, vnt, *([ckt] * npg), *([cvt] * npg))


def _gla_constants():
    c = GLA_CHUNK
    t = np.arange(c)[:, None]
    u = np.arange(c)[None, :]
    blocks = [(u <= t), (u > t)]
    level = np.zeros((c, c), np.int32)
    for l in range(1, GLA_LEVELS + 1):
        m = c >> l
        r = (t // (2 * m)) * (2 * m) + m - 1
        upper = (t % (2 * m)) >= m
        blocks.append(upper & (u > r) & (u <= t))
        blocks.append((~upper) & (u > t) & (u <= r))
        tt, ss = np.arange(c)[:, None], np.arange(c)[None, :]
        same = (tt // (2 * m)) == (ss // (2 * m))
        level[same & ((tt % (2 * m)) >= m) & ((ss % (2 * m)) < m)] = l
    level[np.arange(c), np.arange(c)] = GLA_LEVELS + 1
    mall = np.concatenate([b.astype(np.float32) for b in blocks], axis=0)
    return jnp.asarray(mall, BF16), jnp.asarray(level)


def _gla_kernel(mall_ref, level_ref, qk_ref, v_ref, r_ref, la_ref, s0_ref, gn_ref,
                o_ref, sfin_ref, s_scr):
    ci = pl.program_id(1)

    @pl.when(ci == 0)
    def _():
        s_scr[...] = s0_ref[...]

    for bi in range(qk_ref.shape[0]):
        _gla_chunk(bi, mall_ref, level_ref, qk_ref, v_ref, r_ref, la_ref, gn_ref, o_ref, s_scr)

    @pl.when(ci == pl.num_programs(1) - 1)
    def _():
        sfin_ref[...] = s_scr[...]


def _gla_chunk(bi, mall_ref, level_ref, qk_ref, v_ref, r_ref, la_ref, gn_ref, o_ref, s_scr):
    c = GLA_CHUNK
    lb = qk_ref.shape[1]

    def rows(x):
        if lb == c:
            return x
        return jnp.concatenate([x, jnp.zeros((c - lb, x.shape[1]), x.dtype)], axis=0)

    qk = rows(qk_ref[bi])
    q = qk[:, :GLA_KWIDTH]
    k = qk[:, GLA_KWIDTH:]
    v = rows(v_ref[bi])
    la = rows(la_ref[bi])
    hi, mid, lo = _split3(la)
    mall = mall_ref[...]
    cum = (jnp.dot(mall, hi, preferred_element_type=F32)
           + jnp.dot(mall, mid, preferred_element_type=F32)
           + jnp.dot(mall, lo, preferred_element_type=F32))
    pw = jnp.exp(cum)

    def blk(i):
        return pw[i * c:(i + 1) * c]

    level = level_ref[...]
    lane = lax.broadcasted_iota(jnp.int32, (1, LANES), 1)
    rowi = lax.broadcasted_iota(jnp.int32, (LANES, 1), 0)
    qb = q * blk(0)
    kf = k * blk(1)
    for pp in range(GLA_KWIDTH // LANES):
        sl = slice(pp * LANES, (pp + 1) * LANES)
        s_pair = s_scr[bi, sl, :]
        kft = kf[:, sl].T.astype(BF16)
        decay = jnp.exp(jnp.sum(la[:, sl].T, axis=1, keepdims=True))
        s_new = decay * s_pair
        hms = (lane < GLA_DK, lane >= GLA_DK)
        qm2 = jnp.concatenate([jnp.where(hm, q[:, sl], 0.0) for hm in hms], axis=0)
        level2 = jnp.concatenate([level, level], axis=0)
        att2 = jnp.where(level2 == GLA_LEVELS + 1,
                         lax.dot_general(qm2.astype(BF16), k[:, sl].astype(BF16), _NT,
                                         preferred_element_type=F32), 0.0)
        for l in range(1, GLA_LEVELS + 1):
            pd = blk(2 * l)[:, sl]
            qd = (qm2 * jnp.concatenate([pd, pd], axis=0)).astype(BF16)
            ke = (k[:, sl] * blk(2 * l + 1)[:, sl]).astype(BF16)
            att2 = att2 + jnp.where(level2 == l,
                                    lax.dot_general(qd, ke, _NT, preferred_element_type=F32), 0.0)
        for hh in range(2):
            h = 2 * pp + hh
            hm = hms[hh]
            vh = v[:, h * GLA_DV:(h + 1) * GLA_DV].astype(BF16)
            att = att2[hh * c:(hh + 1) * c]
            o = jnp.dot(att.astype(BF16), vh, preferred_element_type=F32)
            o = o + jnp.dot(jnp.where(hm, qb[:, sl], 0.0).astype(BF16), s_pair.astype(BF16),
                            preferred_element_type=F32)
            o = _rmsnorm(o, gn_ref[:, h * GLA_DV:(h + 1) * GLA_DV])
            rg = rows(r_ref[bi][:, h * GLA_DV:(h + 1) * GLA_DV])
            o = o * (rg * jax.nn.sigmoid(rg))
            o_ref[bi, :, h * GLA_DV:(h + 1) * GLA_DV] = o[:lb]
            rm = (rowi < GLA_DK) if hh == 0 else (rowi >= GLA_DK)
            s_new = s_new + jnp.where(rm, jnp.dot(kft, vh, preferred_element_type=F32), 0.0)
        s_scr[bi, sl, :] = s_new


def _gla(qk, v, r, la, s0, gn):
    b, l, _ = qk.shape
    c = GLA_CHUNK
    lb = c if l % c == 0 else l
    assert lb == c or (l < c and l % SUBLANES == 0)
    mall, level = _gla_constants()
    nb = GLA_SEQS_PER_STEP if b % GLA_SEQS_PER_STEP == 0 else 1
    const = lambda bi, ci: (0, 0)
    tok = lambda bi, ci: (bi, ci, 0)
    seq = lambda bi, ci: (bi, 0, 0)
    return pl.pallas_call(
        _gla_kernel,
        grid=(b // nb, l // lb),
        in_specs=[pl.BlockSpec(mall.shape, const),
                  pl.BlockSpec(level.shape, const),
                  pl.BlockSpec((nb, lb, 2 * GLA_KWIDTH), tok),
                  pl.BlockSpec((nb, lb, GLA_VWIDTH), tok),
                  pl.BlockSpec((nb, lb, GLA_VWIDTH), tok),
                  pl.BlockSpec((nb, lb, GLA_KWIDTH), tok),
                  pl.BlockSpec((nb, GLA_KWIDTH, GLA_DV), seq),
                  pl.BlockSpec((1, GLA_VWIDTH), const)],
        out_specs=[pl.BlockSpec((nb, lb, GLA_VWIDTH), tok),
                   pl.BlockSpec((nb, GLA_KWIDTH, GLA_DV), seq)],
        out_shape=[jax.ShapeDtypeStruct((b, l, GLA_VWIDTH), F32),
                   jax.ShapeDtypeStruct((b, GLA_KWIDTH, GLA_DV), F32)],
        scratch_shapes=[pltpu.VMEM((nb, GLA_KWIDTH, GLA_DV), F32)],
        compiler_params=_cparams(("parallel", "arbitrary")),
        name="gla",
    )(mall, level, qk, v, r, la, s0, gn)


def _merge_kernel(x_ref, oa_ref, ob_ref, gates_ref, wsb_ref, wgl_ref, wout_ref, g2_ref, wq_ref,
                  x1_ref, h2_ref, qry_ref):
    d = x_ref.shape[1]
    ua = jnp.dot(oa_ref[...].astype(BF16), wsb_ref[...], preferred_element_type=F32)
    ub = jnp.dot(ob_ref[...].astype(BF16), wgl_ref[...], preferred_element_type=F32)
    merged = jax.nn.sigmoid(gates_ref[:, :d]) * ua + jax.nn.sigmoid(gates_ref[:, d:]) * ub
    x1 = x_ref[...] + jnp.dot(merged.astype(BF16), wout_ref[...], preferred_element_type=F32)
    x1_ref[...] = x1
    h2 = _rmsnorm(x1, g2_ref[...])
    h2_ref[...] = h2
    qry_ref[...] = jnp.dot(h2.astype(BF16), wq_ref[...], preferred_element_type=F32)


def _merge(x2d, oa, ob, gates, wsb, wgl, wout, g2, wq, tm):
    n, d = x2d.shape
    nq = wq.shape[1]
    const = lambda i: (0, 0)
    row = lambda i: (i, 0)
    return pl.pallas_call(
        _merge_kernel,
        grid=(n // tm,),
        in_specs=[pl.BlockSpec((tm, d), row),
                  pl.BlockSpec((tm, SB_WIDTH), row),
                  pl.BlockSpec((tm, GLA_VWIDTH), row),
                  pl.BlockSpec((tm, 2 * d), row),
                  pl.BlockSpec(wsb.shape, const, pipeline_mode=pl.Buffered(1)),
                  pl.BlockSpec(wgl.shape, const, pipeline_mode=pl.Buffered(1)),
                  pl.BlockSpec(wout.shape, const, pipeline_mode=pl.Buffered(1)),
                  pl.BlockSpec((1, d), const),
                  pl.BlockSpec(wq.shape, const, pipeline_mode=pl.Buffered(1))],
        out_specs=[pl.BlockSpec((tm, d), row), pl.BlockSpec((tm, d), row), pl.BlockSpec((tm, nq), row)],
        out_shape=[jax.ShapeDtypeStruct((n, d), F32), jax.ShapeDtypeStruct((n, d), F32),
                   jax.ShapeDtypeStruct((n, nq), F32)],
        compiler_params=_cparams(("parallel",)),
        name="merge",
    )(x2d, oa, ob, gates, wsb, wgl, wout, g2, wq)


RANK_CODE_BASE = 2.0 ** 126
RANK_CODE_STEP = 2.0 ** 106


def _peer_candidates():
    return [(a, b) for a in range(PEER_TOPK) for b in range(PEER_TOPK) if (a + 1) * (b + 1) <= PEER_TOPK]


def _route_kernel(qry_ref, keys_ref, a0_ref, c0_ref, r1_ref, b1_ref, top_scr, rank_scr):
    neg = -jnp.inf
    for h in range(PEER_HEADS):
        for p in range(2):
            hp = 2 * h + p
            qh = qry_ref[:, hp * PEER_KEY_HALF:(hp + 1) * PEER_KEY_HALF]
            s = lax.dot_general(keys_ref[hp], qh, _NT, preferred_element_type=F32,
                                precision=lax.Precision.HIGHEST)
            cur = s
            for r in range(PEER_TOPK):
                m = jnp.max(cur, axis=0, keepdims=True)
                top_scr[p, r, h:h + 1, :] = m
                cur = jnp.where(cur >= m, -(RANK_CODE_BASE + r * RANK_CODE_STEP), cur)
            rank = jnp.where(cur <= -RANK_CODE_BASE, (-cur - RANK_CODE_BASE) * (1.0 / RANK_CODE_STEP),
                             float(PEER_TOPK))
            e = jnp.exp(s - top_scr[p, 0, h:h + 1, :])
            if p == 0:
                rank_scr[h] = rank
                a0_ref[h] = e
            else:
                r1_ref[h] = rank.astype(BF16)
                b1_ref[h] = e.astype(BF16)
    pairs = _peer_candidates()
    cands = [top_scr[0, a] + top_scr[1, b] for (a, b) in pairs]
    cmax = cands[0]
    zsum = jnp.zeros_like(cmax)
    tau = cmax
    for r in range(PEER_TOPK):
        tau = functools.reduce(jnp.maximum, cands)
        zsum = zsum + jnp.exp(tau - cmax)
        cands = [jnp.where(cd >= tau, neg, cd) for cd in cands]
    a0_scale = (0.5 * 2.0 ** 0.5) / zsum
    cnt = [jnp.zeros_like(cmax) for _ in range(PEER_TOPK)]
    for (a, b) in pairs:
        cnt[a] = cnt[a] + jnp.where(top_scr[0, a] + top_scr[1, b] >= tau, 1.0, 0.0)
    many = PEER_TOPK // 2
    n_active = functools.reduce(lambda x, y: x + y, [jnp.minimum(c, 1.0) for c in cnt])
    for h in range(PEER_HEADS):
        rank = rank_scr[h]
        c0 = jnp.where(rank < n_active[h:h + 1, :], 1.0, 0.0)
        for a in range(many):
            c0 = jnp.where(rank == float(a), cnt[a][h:h + 1, :], c0)
        c0_ref[h] = c0
        a0_ref[h] = a0_ref[h] * a0_scale[h:h + 1, :]


def _route(qry, keys, t):
    n = qry.shape[0]
    nhp = 2 * PEER_HEADS
    blk = pl.BlockSpec((PEER_HEADS, PEER_N_KEYS, t), lambda i: (0, 0, i))
    shp = jax.ShapeDtypeStruct((PEER_HEADS, PEER_N_KEYS, n), F32)
    shp16 = jax.ShapeDtypeStruct((PEER_HEADS, PEER_N_KEYS, n), BF16)
    return pl.pallas_call(
        _route_kernel,
        grid=(n // t,),
        in_specs=[pl.BlockSpec((t, nhp * PEER_KEY_HALF), lambda i: (i, 0)),
                  pl.BlockSpec((nhp, PEER_N_KEYS, PEER_KEY_HALF), lambda i: (0, 0, 0))],
        out_specs=[blk, blk, blk, blk],
        out_shape=[shp, shp, shp16, shp16],
        scratch_shapes=[pltpu.VMEM((2, PEER_TOPK, PEER_HEADS, t), F32),
                        pltpu.VMEM((PEER_HEADS, PEER_N_KEYS, t), F32)],
        compiler_params=_cparams(("parallel",)),
        name="peer_route",
    )(qry, keys)


def _gelu_core(y):
    return y * (1.0 + lax.erf(y))


def _row_tile_bf16(row, n):
    packed_rows = 2 * SUBLANES
    one = jnp.broadcast_to(row, (packed_rows, row.shape[1])).astype(BF16)
    return pltpu.repeat(one, n // packed_rows, axis=0)


def _peer_kernel(h2_ref, x1_ref, a0_ref, c0_ref, r1_ref, b1_ref, u_ref, vt_ref, gf_ref, y_ref,
                 xt_scr, acc_scr, g_scr, *, t):
    j = pl.program_id(1)
    nk = PEER_N_KEYS
    ni = a0_ref.shape[1]

    @pl.when(j == 0)
    def _():
        xt_scr[...] = h2_ref[...].T.astype(BF16)
        acc_scr[...] = jnp.zeros_like(acc_scr)

    act = jnp.dot(u_ref[...], xt_scr[...], preferred_element_type=F32)
    for ii in range(ni):
        for lt in range(t // LANES):
            ls = slice(lt * LANES, (lt + 1) * LANES)
            w = jnp.zeros((nk, LANES), BF16)
            for h in range(PEER_HEADS):
                a0 = _row_tile_bf16(a0_ref[h, ii:ii + 1, ls], nk)
                c0 = _row_tile_bf16(c0_ref[h, ii:ii + 1, ls], nk)
                w = w + (a0 * b1_ref[h, :, ls]) * jnp.clip(c0 - r1_ref[h, :, ls], 0.0, 1.0)
            g_scr[ii * nk:(ii + 1) * nk, ls] = _gelu_core(act[ii * nk:(ii + 1) * nk, ls]).astype(BF16) * w
    acc_scr[...] += jnp.dot(vt_ref[...], g_scr[...], preferred_element_type=F32)

    @pl.when(j == pl.num_programs(1) - 1)
    def _():
        y_ref[...] = _rmsnorm(x1_ref[...] + acc_scr[...].T, gf_ref[...])


def _peer(h2, x1, a0, c0, r1, b1, u_bf, vt_bf, gf, t):
    n, d = h2.shape
    ne = u_bf.shape[0]
    ni = 2 * SUBLANES
    e_tile = ni * PEER_N_KEYS
    half0 = pl.BlockSpec((PEER_HEADS, ni, t), lambda i, j: (0, j, i))
    half1 = pl.BlockSpec((PEER_HEADS, PEER_N_KEYS, t), lambda i, j: (0, 0, i))
    return pl.pallas_call(
        functools.partial(_peer_kernel, t=t),
        grid=(n // t, ne // e_tile),
        in_specs=[pl.BlockSpec((t, d), lambda i, j: (i, 0)),
                  pl.BlockSpec((t, d), lambda i, j: (i, 0)),
                  half0, half0, half1, half1,
                  pl.BlockSpec((e_tile, d), lambda i, j: (j, 0)),
                  pl.BlockSpec((d, e_tile), lambda i, j: (0, j)),
                  pl.BlockSpec((1, d), lambda i, j: (0, 0))],
        out_specs=pl.BlockSpec((t, d), lambda i, j: (i, 0)),
        out_shape=jax.ShapeDtypeStruct((n, d), F32),
        scratch_shapes=[pltpu.VMEM((d, t), BF16),
                        pltpu.VMEM((d, t), F32),
                        pltpu.VMEM((e_tile, t), BF16)],
        compiler_params=_cparams(("parallel", "arbitrary")),
        name="peer_experts",
    )(h2, x1, a0, c0, r1, b1, u_bf, vt_bf, gf)


def _token_tile(n, pref):
    t = pref
    while n % t:
        t //= 2
    return t


def _group(x, sb_fn, s0, wts, kv_t):
    b, l, d = x.shape
    n = b * l
    x2d = x.reshape(n, d)
    tm = _token_tile(l if kv_t else n, PROJ_TOKEN_TILE)
    qa, ka, va, qkb, vb, rb, la, gates = _in_proj(
        x2d, wts['norm_mix_g'], wts['w_main'], wts['w_kvt'], wts['w_lr'], wts['w2'], wts['b2'], tm, l, kv_t)
    r3 = lambda a: a.reshape(b, l, a.shape[-1])
    if not kv_t:
        ka, va = r3(ka), r3(va)
    oa = sb_fn(r3(qa), ka, va)
    ob, s_fin = _gla(r3(qkb), r3(vb), r3(rb), r3(la), s0, wts['gla_norm_g'])
    x1, h2, qry = _merge(x2d, oa.reshape(n, SB_WIDTH), ob.reshape(n, GLA_VWIDTH), gates,
                         wts['w_sb_up'], wts['w_gla_up'], wts['w_out'], wts['norm_ffn_g'],
                         wts['peer_w_q'], _token_tile(n, PROJ_TOKEN_TILE))
    a0, c0, r1, b1 = _route(qry, wts['peer_keys'], _token_tile(n, 256))
    y = _peer(h2, x1, a0, c0, r1, b1, wts['peer_u'], wts['peer_vt'], wts['norm_out_g'],
              _token_tile(n, 512))
    return y.reshape(b, l, d), ka, va, s_fin


def kernel(x_prompt, x_sample, cache_sb_k, cache_sb_v, page_table, state_gla, norm_mix_g, w_in, sb_bias,
           gla_gate_w2, gla_gate_b, gla_norm_g, w_sb_up, w_gla_up, w_out, norm_ffn_g, peer_w_q, peer_keys,
           peer_u, peer_v, norm_final_g):
    depth = w_in.shape[0]
    assert depth == 1, "the final norm is fused into the last (only) layer"
    bp, lp, d = x_prompt.shape
    bs, lsm, _ = x_sample.shape
    xp, xs = x_prompt, x_sample
    outs = [[] for _ in range(6)]
    for l in range(depth):
        w = w_in[l]
        o_lr = 3 * SB_WIDTH + 2 * GLA_KWIDTH + 2 * GLA_VWIDTH
        wts = dict(
            norm_mix_g=norm_mix_g[l].reshape(1, d),
            w_main=jnp.concatenate([w[:, :o_lr], w[:, o_lr + GLA_GATE_RANK:]], axis=1).astype(BF16),
            w_kvt=w[:, SB_WIDTH:3 * SB_WIDTH].T.astype(BF16),
            w_lr=jnp.pad(w[:, o_lr:o_lr + GLA_GATE_RANK], ((0, 0), (0, LANES - GLA_GATE_RANK))).astype(BF16),
            w2=jnp.pad(gla_gate_w2[l], ((0, LANES - GLA_GATE_RANK), (0, 0))),
            b2=gla_gate_b[l].reshape(1, GLA_KWIDTH),
            gla_norm_g=gla_norm_g[l].reshape(1, GLA_VWIDTH),
            w_sb_up=w_sb_up[l].astype(BF16),
            w_gla_up=w_gla_up[l].astype(BF16),
            w_out=w_out[l].astype(BF16),
            norm_ffn_g=norm_ffn_g[l].reshape(1, d),
            peer_w_q=peer_w_q[l].astype(BF16),
            peer_keys=peer_keys[l].reshape(2 * PEER_HEADS, PEER_N_KEYS, PEER_KEY_HALF),
            peer_u=(peer_u[l] * 2.0 ** -0.5).astype(BF16),
            peer_vt=peer_v[l].T.astype(BF16),
            norm_out_g=norm_final_g.reshape(1, d),
        )
        bias = sb_bias[l]

        s0p = jnp.zeros((bp, GLA_KWIDTH, GLA_DV), F32)
        xp, kp, vp, sp = _group(xp, lambda q, k, v: _sb_prompt(q, k, v, bias), s0p, wts, True)
        s0s = state_gla[l].reshape(bs, GLA_KWIDTH, GLA_DV)
        xs, ks, vs, ss = _group(
            xs, lambda q, k, v: _sb_sample(q, k, v, bias, cache_sb_k, cache_sb_v, l, page_table), s0s, wts, False)

        outs[0].append(kp.reshape(bp, SB_HEADS, SB_HEAD_DIM, lp).transpose(0, 3, 1, 2))
        outs[1].append(vp.reshape(bp, SB_HEADS, SB_HEAD_DIM, lp).transpose(0, 3, 1, 2))
        outs[2].append(sp.reshape(bp, GLA_HEADS, GLA_DK, GLA_DV))
        outs[3].append(ks.reshape(bs, lsm, SB_HEADS, SB_HEAD_DIM))
        outs[4].append(vs.reshape(bs, lsm, SB_HEADS, SB_HEAD_DIM))
        outs[5].append(ss.reshape(bs, GLA_HEADS, GLA_DK, GLA_DV))
    return (xp, xs) + tuple(o[0][None] for o in outs)
```

```python
import functools

import numpy as np
import jax
import jax.numpy as jnp
from jax import lax
from jax.experimental import pallas as pl
from jax.experimental.pallas import tpu as pltpu

F32 = jnp.float32
BF16 = jnp.bfloat16

NORM_EPS = 1e-6
LOG2E = 1.4426950408889634
SB_HEADS = 8
SB_HEAD_DIM = 64
SB_WIDTH = SB_HEADS * SB_HEAD_DIM
GLA_HEADS = 4
GLA_DK = 64
GLA_DV = 128
GLA_KWIDTH = GLA_HEADS * GLA_DK
GLA_VWIDTH = GLA_HEADS * GLA_DV
GLA_GATE_RANK = 16
GLA_GATE_TEMP = 16.0
PEER_HEADS = 8
PEER_N_KEYS = 128
PEER_KEY_HALF = 128
PEER_TOPK = 16

LANES = 128
SUBLANES = 8
VMEM_LIMIT_BYTES = 56 * 1024 * 1024
GLA_CHUNK = 128
GLA_LEVELS = 7
GLA_SEQS_PER_STEP = 2
SB_TILE = 256
SB_QUERY_TILE = 512
SB_PAGES_PER_STEP = 32
PROJ_TOKEN_TILE = 512

_NT = (((1,), (1,)), ((), ()))
_TN = (((0,), (0,)), ((), ()))


def _cparams(sem):
    return pltpu.CompilerParams(dimension_semantics=sem, vmem_limit_bytes=VMEM_LIMIT_BYTES)


def _rmsnorm(x, g):
    return x * lax.rsqrt(jnp.mean(x * x, axis=-1, keepdims=True) + NORM_EPS) * g


def _neg_softplus(z):
    return -(jnp.maximum(z, 0.0) + jnp.log1p(jnp.exp(-jnp.abs(z))))


def _stick_log2(z2):
    mn = jnp.minimum(z2, 0.0)
    t = jnp.log2(1.0 + jnp.exp2(mn + mn - z2))
    lsig = mn - t
    return lsig, lsig - z2


def _aligned(i, m):
    return i if isinstance(i, int) else pl.multiple_of(i, m)


def _split2(x):
    hi = x.astype(BF16)
    lo = (x - hi.astype(F32)).astype(BF16)
    return hi, lo


def _split3(x):
    hi = x.astype(BF16)
    r = x - hi.astype(F32)
    mid = r.astype(BF16)
    lo = (r - mid.astype(F32)).astype(BF16)
    return hi, mid, lo


def _inproj_kernel(x_ref, g_ref, w_ref, wkvt_ref, wlr_ref, w2_ref, b2_ref,
                   qa_ref, ka_ref, va_ref, qkb_ref, vb_ref, rb_ref, la_ref, gates_ref, *, kv_t):
    h = _rmsnorm(x_ref[...], g_ref[...]).astype(BF16)

    def proj(lo, hi):
        return jnp.dot(h, w_ref[:, lo:hi], preferred_element_type=F32)

    o = 0
    qa_ref[...] = proj(o, o + SB_WIDTH) * (SB_HEAD_DIM ** -0.5 * LOG2E); o += SB_WIDTH
    if kv_t:
        ka_ref[0] = lax.dot_general(wkvt_ref[:SB_WIDTH, :], h, _NT, preferred_element_type=F32)
        va_ref[0] = lax.dot_general(wkvt_ref[SB_WIDTH:, :], h, _NT, preferred_element_type=F32)
    else:
        ka_ref[...] = proj(o, o + SB_WIDTH)
        va_ref[...] = proj(o + SB_WIDTH, o + 2 * SB_WIDTH)
    o += 2 * SB_WIDTH
    qkb_ref[:, :GLA_KWIDTH] = proj(o, o + GLA_KWIDTH) * (GLA_DK ** -0.5); o += GLA_KWIDTH
    qkb_ref[:, GLA_KWIDTH:] = proj(o, o + GLA_KWIDTH); o += GLA_KWIDTH
    vb_ref[...] = proj(o, o + GLA_VWIDTH); o += GLA_VWIDTH
    rb_ref[...] = proj(o, o + GLA_VWIDTH); o += GLA_VWIDTH
    d = gates_ref.shape[1]
    gates_ref[...] = proj(o, o + d)
    gk = jnp.dot(h, wlr_ref[...], preferred_element_type=F32)
    pre = jnp.dot(gk, w2_ref[...], preferred_element_type=F32,
                  precision=lax.Precision.HIGHEST) + b2_ref[...]
    la_ref[...] = _neg_softplus(-pre) * (1.0 / GLA_GATE_TEMP)


def _in_proj(x2d, g, w_main, w_kvt, w_lr, w2, b2, tm, seq_len, kv_t):
    n, d = x2d.shape
    wm = w_main.shape[1]
    const = lambda i: (0, 0)
    row = lambda i: (i, 0)
    widths = [SB_WIDTH, SB_WIDTH, SB_WIDTH, 2 * GLA_KWIDTH, GLA_VWIDTH, GLA_VWIDTH, GLA_KWIDTH, 2 * d]
    out_specs = [pl.BlockSpec((tm, w), row) for w in widths]
    out_shape = [jax.ShapeDtypeStruct((n, w), F32) for w in widths]
    if kv_t:
        per_seq = seq_len // tm
        for i in (1, 2):
            out_specs[i] = pl.BlockSpec((1, SB_WIDTH, tm), lambda i: (i // per_seq, 0, i % per_seq))
            out_shape[i] = jax.ShapeDtypeStruct((n // seq_len, SB_WIDTH, seq_len), F32)
    return pl.pallas_call(
        functools.partial(_inproj_kernel, kv_t=kv_t),
        grid=(n // tm,),
        in_specs=[pl.BlockSpec((tm, d), row),
                  pl.BlockSpec((1, d), const),
                  pl.BlockSpec((d, wm), const, pipeline_mode=pl.Buffered(1)),
                  pl.BlockSpec(w_kvt.shape, const, pipeline_mode=pl.Buffered(1)),
                  pl.BlockSpec((d, LANES), const),
                  pl.BlockSpec((LANES, GLA_KWIDTH), const),
                  pl.BlockSpec((1, GLA_KWIDTH), const)],
        out_specs=out_specs,
        out_shape=out_shape,
        compiler_params=_cparams(("parallel",)),
        name="in_proj",
    )(x2d, g, w_main, w_kvt, w_lr, w2, b2)


def _sb_prompt_kernel(bias_ref, cmat_ref, q_ref, k_ref, v_ref, o_ref, *, tq, t):
    p = pl.program_id(1)
    qi = pl.program_id(2)
    lane = lax.broadcasted_iota(jnp.int32, (1, LANES), 1)
    first = lane < SB_HEAD_DIM
    first_row = lax.broadcasted_iota(jnp.int32, (LANES, 1), 0) < SB_HEAD_DIM
    q = q_ref[0]
    qh = (jnp.where(first, q, 0.0).astype(BF16), jnp.where(first, 0.0, q).astype(BF16))
    bias = (bias_ref[2 * p], bias_ref[2 * p + 1])
    nd = tq // t

    def block(j, carry, d):
        masked = d is not None
        acc, c0, c1 = carry
        off = _aligned(j * t, t)
        kt = k_ref[0, :, pl.ds(off, t)].astype(BF16)
        vt = v_ref[0, :, pl.ds(off, t)]
        vh = (jnp.where(first_row, vt, 0.0).astype(BF16), jnp.where(first_row, 0.0, vt).astype(BF16))
        if masked:
            valid = (lax.broadcasted_iota(jnp.int32, (tq, t), 1) + d * t
                     < lax.broadcasted_iota(jnp.int32, (tq, t), 0))
        cs = [c0, c1]
        for hh in range(2):
            z = jnp.dot(qh[hh], kt, preferred_element_type=F32) + bias[hh]
            lsig, ls = _stick_log2(z)
            if masked:
                ls = jnp.where(valid, ls, 0.0)
            lsb = ls.astype(BF16)
            r = jnp.dot(lsb, cmat_ref[...], preferred_element_type=F32)
            w = jnp.exp2(lsig + r + cs[hh])
            if masked:
                w = jnp.where(valid, w, 0.0)
            acc = acc + lax.dot_general(w.astype(BF16), vh[hh], _NT, preferred_element_type=F32)
            total = r[:, :1] + lsb[:, :1].astype(F32)
            cs[hh] = cs[hh] + jnp.broadcast_to(total, (tq, t))
        return acc, cs[0], cs[1]

    carry = (jnp.zeros((tq, LANES), F32), jnp.zeros((tq, t), F32), jnp.zeros((tq, t), F32))
    for d in reversed(range(nd)):
        carry = block(qi * nd + d, carry, d)
    carry = lax.fori_loop(0, qi * nd, lambda jj, c: block(qi * nd - 1 - jj, c, None), carry)
    o_ref[0] = carry[0]


def _sb_prompt(q, k, v, bias):
    b, l, _ = q.shape
    t = min(SB_TILE, l)
    tq = min(SB_QUERY_TILE, l)
    cmat = jnp.asarray(np.arange(t)[:, None] > np.arange(t)[None, :], BF16)
    return pl.pallas_call(
        functools.partial(_sb_prompt_kernel, tq=tq, t=t),
        grid=(b, SB_WIDTH // LANES, l // tq),
        in_specs=[pl.BlockSpec(memory_space=pltpu.SMEM),
                  pl.BlockSpec((t, t), lambda bi, p, qi: (0, 0)),
                  pl.BlockSpec((1, tq, LANES), lambda bi, p, qi: (bi, qi, p)),
                  pl.BlockSpec((1, LANES, l), lambda bi, p, qi: (bi, p, 0)),
                  pl.BlockSpec((1, LANES, l), lambda bi, p, qi: (bi, p, 0))],
        out_specs=pl.BlockSpec((1, tq, LANES), lambda bi, p, qi: (bi, qi, p)),
        out_shape=jax.ShapeDtypeStruct((b, l, SB_WIDTH), F32),
        compiler_params=_cparams(("parallel", "parallel", "arbitrary")),
        name="sb_prompt",
    )(bias * LOG2E, cmat, q, k, v)


def _sb_sample_kernel(pt_ref, brow_ref, cmat_ref, qbd_ref, kn_ref, vn_ref, *rest, npg, ln):
    del pt_ref
    kp_refs = rest[:npg]
    vp_refs = rest[npg:2 * npg]
    o_ref, acc_ref, carry_ref = rest[2 * npg:]
    g = pl.program_id(1)
    pg = kp_refs[0].shape[3]
    qbd = qbd_ref[0]
    bcol = brow_ref[...]

    nr = SB_HEADS * ln

    def pages(kts, vts, valid):
        zs = [jnp.dot(qbd, kt.astype(BF16), preferred_element_type=F32) + bcol for kt in kts]
        lsigs, lss = zip(*[_stick_log2(z) for z in zs])
        if valid is not None:
            lss = [jnp.where(valid, ls, 0.0) for ls in lss]
        r = jnp.dot(jnp.concatenate(lss, axis=0).astype(BF16), cmat_ref[...], preferred_element_type=F32)
        run = carry_ref[...]
        acc = acc_ref[...]
        for i, (lsig, vt) in enumerate(zip(lsigs, vts)):
            w = jnp.exp2(lsig + r[i * nr:(i + 1) * nr, :pg] + run)
            if valid is not None:
                w = jnp.where(valid, w, 0.0)
            acc = acc + lax.dot_general(w.astype(BF16), vt.astype(BF16), _NT, preferred_element_type=F32)
            run = run + r[i * nr:(i + 1) * nr, pg:]
        acc_ref[...] = acc
        carry_ref[...] = run

    @pl.when(g == 0)
    def _():
        acc_ref[...] = jnp.zeros_like(acc_ref)
        carry_ref[...] = jnp.zeros_like(carry_ref)
        s_lane = lax.broadcasted_iota(jnp.int32, (nr, pg), 1)
        t_row = lax.broadcasted_iota(jnp.int32, (nr, pg), 0) % ln
        pages([kn_ref[0]], [vn_ref[0]], s_lane < t_row)

    pages([r_[0, 0] for r_ in kp_refs], [r_[0, 0] for r_ in vp_refs], None)

    @pl.when(g == pl.num_programs(1) - 1)
    def _():
        lane = lax.broadcasted_iota(jnp.int32, (ln, SB_WIDTH), 1)
        out = jnp.zeros((ln, SB_WIDTH), F32)
        for h in range(SB_HEADS):
            out = out + jnp.where(lane // SB_HEAD_DIM == h, acc_ref[h * ln:(h + 1) * ln, :], 0.0)
        o_ref[0] = out


def _sb_sample(q, k_new, v_new, bias, cache_k, cache_v, layer, page_table):
    bd, ln, _ = q.shape
    n_pages = page_table.shape[1]
    pg = cache_k.shape[2]
    npg = SB_PAGES_PER_STEP
    while n_pages % npg:
        npg //= 2
    depth, n_pool = cache_k.shape[:2]
    nr = SB_HEADS * ln
    assert ln == SUBLANES and ln <= pg
    q4 = q.reshape(bd, ln, SB_HEADS, SB_HEAD_DIM)
    qbd = jnp.einsum('bthd,hg->bhtgd', q4, jnp.eye(SB_HEADS, dtype=F32)).reshape(bd, nr, SB_WIDTH).astype(BF16)
    bcol = jnp.broadcast_to(jnp.repeat(bias * LOG2E, ln)[:, None], (nr, pg))
    later = np.arange(pg)[:, None] > np.arange(pg)[None, :]
    cmat = jnp.asarray(np.concatenate([later, np.ones((pg, pg), bool)], axis=1), BF16)
    ckt = cache_k.transpose(0, 1, 3, 4, 2).reshape(depth, n_pool, SB_WIDTH, pg)
    cvt = cache_v.transpose(0, 1, 3, 4, 2).reshape(depth, n_pool, SB_WIDTH, pg)
    knt = jnp.pad(k_new.transpose(0, 2, 1), ((0, 0), (0, 0), (0, pg - ln)))
    vnt = jnp.pad(v_new.transpose(0, 2, 1), ((0, 0), (0, 0), (0, pg - ln)))

    def page_spec(i):
        return pl.BlockSpec((1, 1, SB_WIDTH, pg),
                            lambda b, g, pt, i=i: (layer, pt[b, n_pages - 1 - (g * npg + i)], 0, 0))

    seq = lambda b, g, pt: (b, 0, 0)
    const = lambda b, g, pt: (0, 0)
    grid_spec = pltpu.PrefetchScalarGridSpec(
        num_scalar_prefetch=1,
        grid=(bd, n_pages // npg),
        in_specs=[pl.BlockSpec((nr, pg), const),
                  pl.BlockSpec((pg, 2 * pg), const),
                  pl.BlockSpec((1, nr, SB_WIDTH), seq),
                  pl.BlockSpec((1, SB_WIDTH, pg), seq),
                  pl.BlockSpec((1, SB_WIDTH, pg), seq)]
                 + [page_spec(i) for i in range(npg)] * 2,
        out_specs=pl.BlockSpec((1, ln, SB_WIDTH), seq),
        scratch_shapes=[pltpu.VMEM((nr, SB_WIDTH), F32), pltpu.VMEM((nr, pg), F32)],
    )
    return pl.pallas_call(
        functools.partial(_sb_sample_kernel, npg=npg, ln=ln),
        grid_spec=grid_spec,
        out_shape=jax.ShapeDtypeStruct((bd, ln, SB_WIDTH), F32),
        compiler_params=_cparams(("parallel", "arbitrary")),
        name="sb_sample",
    )(page_table, bcol, cmat, qbd, knt, vnt, *([ckt] * npg), *([cvt] * npg))


def _gla_constants():
    c = GLA_CHUNK
    t = np.arange(c)[:, None]
    u = np.arange(c)[None, :]
    blocks = [(u <= t), (u > t)]
    level = np.zeros((c, c), np.int32)
    for l in range(1, GLA_LEVELS + 1):
        m = c >> l
        r = (t // (2 * m)) * (2 * m) + m - 1
        upper = (t % (2 * m)) >= m
        blocks.append(upper & (u > r) & (u <= t))
        blocks.append((~upper) & (u > t) & (u <= r))
        tt, ss = np.arange(c)[:, None], np.arange(c)[None, :]
        same = (tt // (2 * m)) == (ss // (2 * m))
        level[same & ((tt % (2 * m)) >= m) & ((ss % (2 * m)) < m)] = l
    level[np.arange(c), np.arange(c)] = GLA_LEVELS + 1
    mall = np.concatenate([b.astype(np.float32) for b in blocks], axis=0)
    return jnp.asarray(mall, BF16), jnp.asarray(level)


def _gla_kernel(mall_ref, level_ref, qk_ref, v_ref, r_ref, la_ref, s0_ref, gn_ref,
                o_ref, sfin_ref, s_scr):
    ci = pl.program_id(1)

    @pl.when(ci == 0)
    def _():
        s_scr[...] = s0_ref[...]

    for bi in range(qk_ref.shape[0]):
        _gla_chunk(bi, mall_ref, level_ref, qk_ref, v_ref, r_ref, la_ref, gn_ref, o_ref, s_scr)

    @pl.when(ci == pl.num_programs(1) - 1)
    def _():
        sfin_ref[...] = s_scr[...]


def _gla_chunk(bi, mall_ref, level_ref, qk_ref, v_ref, r_ref, la_ref, gn_ref, o_ref, s_scr):
    c = GLA_CHUNK
    lb = qk_ref.shape[1]

    def rows(x):
        if lb == c:
            return x
        return jnp.concatenate([x, jnp.zeros((c - lb, x.shape[1]), x.dtype)], axis=0)

    qk = rows(qk_ref[bi])
    q = qk[:, :GLA_KWIDTH]
    k = qk[:, GLA_KWIDTH:]
    v = rows(v_ref[bi])
    la = rows(la_ref[bi])
    hi, mid, lo = _split3(la)
    mall = mall_ref[...]
    cum = (jnp.dot(mall, hi, preferred_element_type=F32)
           + jnp.dot(mall, mid, preferred_element_type=F32)
           + jnp.dot(mall, lo, preferred_element_type=F32))
    pw = jnp.exp(cum)

    def blk(i):
        return pw[i * c:(i + 1) * c]

    level = level_ref[...]
    lane = lax.broadcasted_iota(jnp.int32, (1, LANES), 1)
    rowi = lax.broadcasted_iota(jnp.int32, (LANES, 1), 0)
    qb = q * blk(0)
    kf = k * blk(1)
    for pp in range(GLA_KWIDTH // LANES):
        sl = slice(pp * LANES, (pp + 1) * LANES)
        s_pair = s_scr[bi, sl, :]
        kft = kf[:, sl].T.astype(BF16)
        decay = jnp.exp(jnp.sum(la[:, sl].T, axis=1, keepdims=True))
        s_new = decay * s_pair
        hms = (lane < GLA_DK, lane >= GLA_DK)
        qm2 = jnp.concatenate([jnp.where(hm, q[:, sl], 0.0) for hm in hms], axis=0)
        level2 = jnp.concatenate([level, level], axis=0)
        att2 = jnp.where(level2 == GLA_LEVELS + 1,
                         lax.dot_general(qm2.astype(BF16), k[:, sl].astype(BF16), _NT,
                                         preferred_element_type=F32), 0.0)
        for l in range(1, GLA_LEVELS + 1):
            pd = blk(2 * l)[:, sl]
            qd = (qm2 * jnp.concatenate([pd, pd], axis=0)).astype(BF16)
            ke = (k[:, sl] * blk(2 * l + 1)[:, sl]).astype(BF16)
            att2 = att2 + jnp.where(level2 == l,
                                    lax.dot_general(qd, ke, _NT, preferred_element_type=F32), 0.0)
        for hh in range(2):
            h = 2 * pp + hh
            hm = hms[hh]
            vh = v[:, h * GLA_DV:(h + 1) * GLA_DV].astype(BF16)
            att = att2[hh * c:(hh + 1) * c]
            o = jnp.dot(att.astype(BF16), vh, preferred_element_type=F32)
            o = o + jnp.dot(jnp.where(hm, qb[:, sl], 0.0).astype(BF16), s_pair.astype(BF16),
                            preferred_element_type=F32)
            o = _rmsnorm(o, gn_ref[:, h * GLA_DV:(h + 1) * GLA_DV])
            rg = rows(r_ref[bi][:, h * GLA_DV:(h + 1) * GLA_DV])
            o = o * (rg * jax.nn.sigmoid(rg))
            o_ref[bi, :, h * GLA_DV:(h + 1) * GLA_DV] = o[:lb]
            rm = (rowi < GLA_DK) if hh == 0 else (rowi >= GLA_DK)
            s_new = s_new + jnp.where(rm, jnp.dot(kft, vh, preferred_element_type=F32), 0.0)
        s_scr[bi, sl, :] = s_new


def _gla(qk, v, r, la, s0, gn):
    b, l, _ = qk.shape
    c = GLA_CHUNK
    lb = c if l % c == 0 else l
    assert lb == c or (l < c and l % SUBLANES == 0)
    mall, level = _gla_constants()
    nb = GLA_SEQS_PER_STEP if b % GLA_SEQS_PER_STEP == 0 else 1
    const = lambda bi, ci: (0, 0)
    tok = lambda bi, ci: (bi, ci, 0)
    seq = lambda bi, ci: (bi, 0, 0)
    return pl.pallas_call(
        _gla_kernel,
        grid=(b // nb, l // lb),
        in_specs=[pl.BlockSpec(mall.shape, const),
                  pl.BlockSpec(level.shape, const),
                  pl.BlockSpec((nb, lb, 2 * GLA_KWIDTH), tok),
                  pl.BlockSpec((nb, lb, GLA_VWIDTH), tok),
                  pl.BlockSpec((nb, lb, GLA_VWIDTH), tok),
                  pl.BlockSpec((nb, lb, GLA_KWIDTH), tok),
                  pl.BlockSpec((nb, GLA_KWIDTH, GLA_DV), seq),
                  pl.BlockSpec((1, GLA_VWIDTH), const)],
        out_specs=[pl.BlockSpec((nb, lb, GLA_VWIDTH), tok),
                   pl.BlockSpec((nb, GLA_KWIDTH, GLA_DV), seq)],
        out_shape=[jax.ShapeDtypeStruct((b, l, GLA_VWIDTH), F32),
                   jax.ShapeDtypeStruct((b, GLA_KWIDTH, GLA_DV), F32)],
        scratch_shapes=[pltpu.VMEM((nb, GLA_KWIDTH, GLA_DV), F32)],
        compiler_params=_cparams(("parallel", "arbitrary")),
        name="gla",
    )(mall, level, qk, v, r, la, s0, gn)


def _merge_kernel(x_ref, oa_ref, ob_ref, gates_ref, wsb_ref, wgl_ref, wout_ref, g2_ref, wq_ref,
                  x1_ref, h2_ref, qry_ref):
    d = x_ref.shape[1]
    ua = jnp.dot(oa_ref[...].astype(BF16), wsb_ref[...], preferred_element_type=F32)
    ub = jnp.dot(ob_ref[...].astype(BF16), wgl_ref[...], preferred_element_type=F32)
    merged = jax.nn.sigmoid(gates_ref[:, :d]) * ua + jax.nn.sigmoid(gates_ref[:, d:]) * ub
    x1 = x_ref[...] + jnp.dot(merged.astype(BF16), wout_ref[...], preferred_element_type=F32)
    x1_ref[...] = x1
    h2 = _rmsnorm(x1, g2_ref[...])
    h2_ref[...] = h2
    qry_ref[...] = jnp.dot(h2.astype(BF16), wq_ref[...], preferred_element_type=F32)


def _merge(x2d, oa, ob, gates, wsb, wgl, wout, g2, wq, tm):
    n, d = x2d.shape
    nq = wq.shape[1]
    const = lambda i: (0, 0)
    row = lambda i: (i, 0)
    return pl.pallas_call(
        _merge_kernel,
        grid=(n // tm,),
        in_specs=[pl.BlockSpec((tm, d), row),
                  pl.BlockSpec((tm, SB_WIDTH), row),
                  pl.BlockSpec((tm, GLA_VWIDTH), row),
                  pl.BlockSpec((tm, 2 * d), row),
                  pl.BlockSpec(wsb.shape, const, pipeline_mode=pl.Buffered(1)),
                  pl.BlockSpec(wgl.shape, const, pipeline_mode=pl.Buffered(1)),
                  pl.BlockSpec(wout.shape, const, pipeline_mode=pl.Buffered(1)),
                  pl.BlockSpec((1, d), const),
                  pl.BlockSpec(wq.shape, const, pipeline_mode=pl.Buffered(1))],
        out_specs=[pl.BlockSpec((tm, d), row), pl.BlockSpec((tm, d), row), pl.BlockSpec((tm, nq), row)],
        out_shape=[jax.ShapeDtypeStruct((n, d), F32), jax.ShapeDtypeStruct((n, d), F32),
                   jax.ShapeDtypeStruct((n, nq), F32)],
        compiler_params=_cparams(("parallel",)),
        name="merge",
    )(x2d, oa, ob, gates, wsb, wgl, wout, g2, wq)


RANK_CODE_BASE = 2.0 ** 126
RANK_CODE_STEP = 2.0 ** 106


def _peer_candidates():
    return [(a, b) for a in range(PEER_TOPK) for b in range(PEER_TOPK) if (a + 1) * (b + 1) <= PEER_TOPK]


def _route_kernel(qry_ref, keys_ref, a0_ref, c0_ref, r1_ref, b1_ref, top_scr, rank_scr):
    neg = -jnp.inf
    for h in range(PEER_HEADS):
        for p in range(2):
            hp = 2 * h + p
            qh = qry_ref[:, hp * PEER_KEY_HALF:(hp + 1) * PEER_KEY_HALF]
            s = lax.dot_general(keys_ref[hp], qh, _NT, preferred_element_type=F32,
                                precision=lax.Precision.HIGHEST)
            cur = s
            for r in range(PEER_TOPK):
                m = jnp.max(cur, axis=0, keepdims=True)
                top_scr[p, r, h:h + 1, :] = m
                cur = jnp.where(cur >= m, -(RANK_CODE_BASE + r * RANK_CODE_STEP), cur)
            rank = jnp.where(cur <= -RANK_CODE_BASE, (-cur - RANK_CODE_BASE) * (1.0 / RANK_CODE_STEP),
                             float(PEER_TOPK))
            e = jnp.exp(s - top_scr[p, 0, h:h + 1, :])
            if p == 0:
                rank_scr[h] = rank
                a0_ref[h] = e
            else:
                r1_ref[h] = rank.astype(BF16)
                b1_ref[h] = e.astype(BF16)
    pairs = _peer_candidates()
    cands = [top_scr[0, a] + top_scr[1, b] for (a, b) in pairs]
    cmax = cands[0]
    zsum = jnp.zeros_like(cmax)
    tau = cmax
    for r in range(PEER_TOPK):
        tau = functools.reduce(jnp.maximum, cands)
        zsum = zsum + jnp.exp(tau - cmax)
        cands = [jnp.where(cd >= tau, neg, cd) for cd in cands]
    a0_scale = (0.5 * 2.0 ** 0.5) / zsum
    cnt = [jnp.zeros_like(cmax) for _ in range(PEER_TOPK)]
    for (a, b) in pairs:
        cnt[a] = cnt[a] + jnp.where(top_scr[0, a] + top_scr[1, b] >= tau, 1.0, 0.0)
    many = PEER_TOPK // 2
    n_active = functools.reduce(lambda x, y: x + y, [jnp.minimum(c, 1.0) for c in cnt])
    for h in range(PEER_HEADS):
        rank = rank_scr[h]
        c0 = jnp.where(rank < n_active[h:h + 1, :], 1.0, 0.0)
        for a in range(many):
            c0 = jnp.where(rank == float(a), cnt[a][h:h + 1, :], c0)
        c0_ref[h] = c0
        a0_ref[h] = a0_ref[h] * a0_scale[h:h + 1, :]


def _route(qry, keys, t):
    n = qry.shape[0]
    nhp = 2 * PEER_HEADS
    blk = pl.BlockSpec((PEER_HEADS, PEER_N_KEYS, t), lambda i: (0, 0, i))
    shp = jax.ShapeDtypeStruct((PEER_HEADS, PEER_N_KEYS, n), F32)
    shp16 = jax.ShapeDtypeStruct((PEER_HEADS, PEER_N_KEYS, n), BF16)
    return pl.pallas_call(
        _route_kernel,
        grid=(n // t,),
        in_specs=[pl.BlockSpec((t, nhp * PEER_KEY_HALF), lambda i: (i, 0)),
                  pl.BlockSpec((nhp, PEER_N_KEYS, PEER_KEY_HALF), lambda i: (0, 0, 0))],
        out_specs=[blk, blk, blk, blk],
        out_shape=[shp, shp, shp16, shp16],
        scratch_shapes=[pltpu.VMEM((2, PEER_TOPK, PEER_HEADS, t), F32),
                        pltpu.VMEM((PEER_HEADS, PEER_N_KEYS, t), F32)],
        compiler_params=_cparams(("parallel",)),
        name="peer_route",
    )(qry, keys)


def _gelu_core(y):
    return y * (1.0 + lax.erf(y))


def _row_tile_bf16(row, n):
    packed_rows = 2 * SUBLANES
    one = jnp.broadcast_to(row, (packed_rows, row.shape[1])).astype(BF16)
    return pltpu.repeat(one, n // packed_rows, axis=0)


def _peer_kernel(h2_ref, x1_ref, a0_ref, c0_ref, r1_ref, b1_ref, u_ref, vt_ref, gf_ref, y_ref,
                 xt_scr, acc_scr, g_scr, *, t):
    j = pl.program_id(1)
    nk = PEER_N_KEYS
    ni = a0_ref.shape[1]

    @pl.when(j == 0)
    def _():
        xt_scr[...] = h2_ref[...].T.astype(BF16)
        acc_scr[...] = jnp.zeros_like(acc_scr)

    act = jnp.dot(u_ref[...], xt_scr[...], preferred_element_type=F32)
    for ii in range(ni):
        for lt in range(t // LANES):
            ls = slice(lt * LANES, (lt + 1) * LANES)
            w = jnp.zeros((nk, LANES), BF16)
            for h in range(PEER_HEADS):
                a0 = _row_tile_bf16(a0_ref[h, ii:ii + 1, ls], nk)
                c0 = _row_tile_bf16(c0_ref[h, ii:ii + 1, ls], nk)
                w = w + (a0 * b1_ref[h, :, ls]) * jnp.clip(c0 - r1_ref[h, :, ls], 0.0, 1.0)
            g_scr[ii * nk:(ii + 1) * nk, ls] = _gelu_core(act[ii * nk:(ii + 1) * nk, ls]).astype(BF16) * w
    acc_scr[...] += jnp.dot(vt_ref[...], g_scr[...], preferred_element_type=F32)

    @pl.when(j == pl.num_programs(1) - 1)
    def _():
        y_ref[...] = _rmsnorm(x1_ref[...] + acc_scr[...].T, gf_ref[...])


def _peer(h2, x1, a0, c0, r1, b1, u_bf, vt_bf, gf, t):
    n, d = h2.shape
    ne = u_bf.shape[0]
    assert t % LANES == 0, "the expert kernel walks whole lane tiles of tokens"
    ni = 2 * SUBLANES
    e_tile = ni * PEER_N_KEYS
    half0 = pl.BlockSpec((PEER_HEADS, ni, t), lambda i, j: (0, j, i))
    half1 = pl.BlockSpec((PEER_HEADS, PEER_N_KEYS, t), lambda i, j: (0, 0, i))
    return pl.pallas_call(
        functools.partial(_peer_kernel, t=t),
        grid=(n // t, ne // e_tile),
        in_specs=[pl.BlockSpec((t, d), lambda i, j: (i, 0)),
                  pl.BlockSpec((t, d), lambda i, j: (i, 0)),
                  half0, half0, half1, half1,
                  pl.BlockSpec((e_tile, d), lambda i, j: (j, 0)),
                  pl.BlockSpec((d, e_tile), lambda i, j: (0, j)),
                  pl.BlockSpec((1, d), lambda i, j: (0, 0))],
        out_specs=pl.BlockSpec((t, d), lambda i, j: (i, 0)),
        out_shape=jax.ShapeDtypeStruct((n, d), F32),
        scratch_shapes=[pltpu.VMEM((d, t), BF16),
                        pltpu.VMEM((d, t), F32),
                        pltpu.VMEM((e_tile, t), BF16)],
        compiler_params=_cparams(("parallel", "arbitrary")),
        name="peer_experts",
    )(h2, x1, a0, c0, r1, b1, u_bf, vt_bf, gf)


def _token_tile(n, pref):
    t = pref
    while n % t:
        t //= 2
    return t


def _group(x, sb_fn, s0, wts, kv_t):
    b, l, d = x.shape
    n = b * l
    x2d = x.reshape(n, d)
    tm = _token_tile(l if kv_t else n, PROJ_TOKEN_TILE)
    qa, ka, va, qkb, vb, rb, la, gates = _in_proj(
        x2d, wts['norm_mix_g'], wts['w_main'], wts['w_kvt'], wts['w_lr'], wts['w2'], wts['b2'], tm, l, kv_t)
    r3 = lambda a: a.reshape(b, l, a.shape[-1])
    if not kv_t:
        ka, va = r3(ka), r3(va)
    oa = sb_fn(r3(qa), ka, va)
    ob, s_fin = _gla(r3(qkb), r3(vb), r3(rb), r3(la), s0, wts['gla_norm_g'])
    x1, h2, qry = _merge(x2d, oa.reshape(n, SB_WIDTH), ob.reshape(n, GLA_VWIDTH), gates,
                         wts['w_sb_up'], wts['w_gla_up'], wts['w_out'], wts['norm_ffn_g'],
                         wts['peer_w_q'], _token_tile(n, PROJ_TOKEN_TILE))
    a0, c0, r1, b1 = _route(qry, wts['peer_keys'], _token_tile(n, 256))
    y = _peer(h2, x1, a0, c0, r1, b1, wts['peer_u'], wts['peer_vt'], wts['norm_out_g'],
              _token_tile(n, 512))
    return y.reshape(b, l, d), ka, va, s_fin


def kernel(x_prompt, x_sample, cache_sb_k, cache_sb_v, page_table, state_gla, norm_mix_g, w_in, sb_bias,
           gla_gate_w2, gla_gate_b, gla_norm_g, w_sb_up, w_gla_up, w_out, norm_ffn_g, peer_w_q, peer_keys,
           peer_u, peer_v, norm_final_g):
    depth = w_in.shape[0]
    assert depth == 1, "the final norm is fused into the last (only) layer"
    bp, lp, d = x_prompt.shape
    bs, lsm, _ = x_sample.shape
    xp, xs = x_prompt, x_sample
    outs = [[] for _ in range(6)]
    for l in range(depth):
        w = w_in[l]
        o_lr = 3 * SB_WIDTH + 2 * GLA_KWIDTH + 2 * GLA_VWIDTH
        wts = dict(
            norm_mix_g=norm_mix_g[l].reshape(1, d),
            w_main=jnp.concatenate([w[:, :o_lr], w[:, o_lr + GLA_GATE_RANK:]], axis=1).astype(BF16),
            w_kvt=w[:, SB_WIDTH:3 * SB_WIDTH].T.astype(BF16),
            w_lr=jnp.pad(w[:, o_lr:o_lr + GLA_GATE_RANK], ((0, 0), (0, LANES - GLA_GATE_RANK))).astype(BF16),
            w2=jnp.pad(gla_gate_w2[l], ((0, LANES - GLA_GATE_RANK), (0, 0))),
            b2=gla_gate_b[l].reshape(1, GLA_KWIDTH),
            gla_norm_g=gla_norm_g[l].reshape(1, GLA_VWIDTH),
            w_sb_up=w_sb_up[l].astype(BF16),
            w_gla_up=w_gla_up[l].astype(BF16),
            w_out=w_out[l].astype(BF16),
            norm_ffn_g=norm_ffn_g[l].reshape(1, d),
            peer_w_q=peer_w_q[l].astype(BF16),
            peer_keys=peer_keys[l].reshape(2 * PEER_HEADS, PEER_N_KEYS, PEER_KEY_HALF),
            peer_u=(peer_u[l] * 2.0 ** -0.5).astype(BF16),
            peer_vt=peer_v[l].T.astype(BF16),
            norm_out_g=norm_final_g.reshape(1, d),
        )
        bias = sb_bias[l]

        s0p = jnp.zeros((bp, GLA_KWIDTH, GLA_DV), F32)
        xp, kp, vp, sp = _group(xp, lambda q, k, v: _sb_prompt(q, k, v, bias), s0p, wts, True)
        s0s = state_gla[l].reshape(bs, GLA_KWIDTH, GLA_DV)
        xs, ks, vs, ss = _group(
            xs, lambda q, k, v: _sb_sample(q, k, v, bias, cache_sb_k, cache_sb_v, l, page_table), s0s, wts, False)

        outs[0].append(kp.reshape(bp, SB_HEADS, SB_HEAD_DIM, lp).transpose(0, 3, 1, 2))
        outs[1].append(vp.reshape(bp, SB_HEADS, SB_HEAD_DIM, lp).transpose(0, 3, 1, 2))
        outs[2].append(sp.reshape(bp, GLA_HEADS, GLA_DK, GLA_DV))
        outs[3].append(ks.reshape(bs, lsm, SB_HEADS, SB_HEAD_DIM))
        outs[4].append(vs.reshape(bs, lsm, SB_HEADS, SB_HEAD_DIM))
        outs[5].append(ss.reshape(bs, GLA_HEADS, GLA_DK, GLA_DV))
    return (xp, xs) + tuple(o[0][None] for o in outs)
```
